```python
import jax, jax.numpy as jnp
from jax import lax
import numpy as np

D_MODEL = 4096
BATCH = 1
SEQ = 16384
DEPTH = 4

N_MIXERS = 3
EPS = 1e-6
D_FF = ((8 * D_MODEL // 3 + 255) // 256) * 256
CONV_WIDTH = 31
LRU_WIDTH = ((4 * D_MODEL // 3 + 255) // 256) * 256
LRU_HEADS = 16
LRU_HEAD_DIM = LRU_WIDTH // LRU_HEADS
LRU_CONV_WIDTH = 4
LRU_C = 8.0
MLSTM_HEADS = 8
MLSTM_QK_DIM = D_MODEL // 2 // MLSTM_HEADS
MLSTM_V_DIM = D_MODEL // MLSTM_HEADS
MLSTM_QK = MLSTM_HEADS * MLSTM_QK_DIM
MLSTM_V = MLSTM_HEADS * MLSTM_V_DIM
MLSTM_PROJ = 2 * MLSTM_QK + 2 * MLSTM_V + 2 * MLSTM_HEADS
MLSTM_CHUNK = 64
N_CONV_LAYERS = (DEPTH + 2) // N_MIXERS
N_LRU_LAYERS = (DEPTH + 1) // N_MIXERS
N_MLSTM_LAYERS = DEPTH // N_MIXERS

kernel_name = "hybrid_conformer_rglru_mlstm_trunk"


def rms_norm(x, g):
    xf = x.astype(jnp.float32)
    y = xf * lax.rsqrt(jnp.mean(xf * xf, axis=-1, keepdims=True) + EPS)
    return (y * g.astype(jnp.float32)).astype(x.dtype)


def layer_norm(x, g, b):
    xf = x.astype(jnp.float32)
    mu = jnp.mean(xf, axis=-1, keepdims=True)
    xc = xf - mu
    y = xc * lax.rsqrt(jnp.mean(xc * xc, axis=-1, keepdims=True) + EPS)
    return (y * g.astype(jnp.float32) + b.astype(jnp.float32)).astype(x.dtype)


def causal_depthwise_conv(u, w, b):
    k_w = w.shape[0]
    out = lax.conv_general_dilated(
        u, w[:, None, :].astype(u.dtype), window_strides=(1,), padding=[(k_w - 1, 0)],
        dimension_numbers=('NWC', 'WIO', 'NWC'), feature_group_count=u.shape[-1])
    return out + b


def conformer_conv_module(x, w_in, b_in, dw, dw_b, ln_g, ln_b, w_out, b_out):
    h = jnp.einsum('bsd,de->bse', x, w_in) + b_in
    val, gate = jnp.split(h, 2, axis=-1)
    h = val * jax.nn.sigmoid(gate)
    h = causal_depthwise_conv(h, dw, dw_b)
    h = jax.nn.silu(layer_norm(h, ln_g, ln_b))
    return jnp.einsum('bsd,de->bse', h, w_out) + b_out


def linear_scan(a, b):
    def combine(c1, c2):
        a1, b1 = c1
        a2, b2 = c2
        return a1 * a2, a2 * b1 + b2
    _, h = lax.associative_scan(combine, (a, b), axis=1)
    return h


def rglru_block(x, w_in, b_in, conv_w, conv_b, gate_a_w, gate_x_w, gate_a_b, gate_x_b,
                lam, w_out, b_out):
    bsz, seq, _ = x.shape
    h = jnp.einsum('bsd,de->bse', x, w_in) + b_in
    gate_branch, rec = jnp.split(h, 2, axis=-1)
    u = causal_depthwise_conv(rec, conv_w, conv_b)
    uh = u.reshape(bsz, seq, LRU_HEADS, LRU_HEAD_DIM)
    r = jax.nn.sigmoid(jnp.einsum('bshi,hij->bshj', uh, gate_a_w).reshape(bsz, seq, LRU_WIDTH) + gate_a_b)
    ig = jax.nn.sigmoid(jnp.einsum('bshi,hij->bshj', uh, gate_x_w).reshape(bsz, seq, LRU_WIDTH) + gate_x_b)
    log_a = -LRU_C * r.astype(jnp.float32) * jax.nn.softplus(-lam.astype(jnp.float32))
    a = jnp.exp(log_a)
    mult = jnp.sqrt(-jnp.expm1(2.0 * log_a))
    hseq = linear_scan(a, mult * (ig * u).astype(jnp.float32)).astype(x.dtype)
    y = hseq * jax.nn.gelu(gate_branch)
    return jnp.einsum('bse,ed->bsd', y, w_out) + b_out


def mlstm_chunkwise(q, k, v, li, lf):
    bsz, seq, n_heads, _ = q.shape
    L = MLSTM_CHUNK
    nc = seq // L

    def to_chunks(t):
        t = t.astype(jnp.float32).reshape((bsz, nc, L) + t.shape[2:])
        return jnp.swapaxes(jnp.moveaxis(t, 1, 0), 2, 3)

    causal = jnp.tril(jnp.ones((L, L), dtype=bool))

    def step(carry, inp):
        c_st, n_st, m_st = carry
        qc, kc, vc, lic, lfc = inp
        b = jnp.cumsum(lfc, axis=-1)
        d_mat = jnp.where(causal, b[..., :, None] - b[..., None, :] + lic[..., None, :], -jnp.inf)
        m_inter = b + m_st[..., None]
        m_out = jnp.maximum(m_inter, jnp.max(d_mat, axis=-1))
        w_inter = jnp.exp(m_inter - m_out)
        s = jnp.einsum('bhld,bhsd->bhls', qc, kc) * jnp.exp(d_mat - m_out[..., None])
        num = w_inter[..., None] * jnp.einsum('bhld,bhde->bhle', qc, c_st) + jnp.einsum('bhls,bhse->bhle', s, vc)
        den = w_inter * jnp.einsum('bhld,bhd->bhl', qc, n_st) + jnp.sum(s, axis=-1)
        h = num / jnp.maximum(jnp.abs(den), jnp.exp(-m_out))[..., None]
        b_last = b[..., -1]
        a_s = b_last[..., None] - b + lic
        m_new = jnp.maximum(b_last + m_st, jnp.max(a_s, axis=-1))
        w_old = jnp.exp(b_last + m_st - m_new)
        w_s = jnp.exp(a_s - m_new[..., None])
        c_new = w_old[..., None, None] * c_st + jnp.einsum('bhs,bhsd,bhse->bhde', w_s, kc, vc)
        n_new = w_old[..., None] * n_st + jnp.einsum('bhs,bhsd->bhd', w_s, kc)
        return (c_new, n_new, m_new), h

    init = (jnp.zeros((bsz, n_heads, q.shape[-1], v.shape[-1]), jnp.float32),
            jnp.zeros((bsz, n_heads, q.shape[-1]), jnp.float32),
            jnp.zeros((bsz, n_heads), jnp.float32))
    _, h = lax.scan(step, init, (to_chunks(q), to_chunks(k), to_chunks(v), to_chunks(li), to_chunks(lf)))
    h = jnp.moveaxis(jnp.swapaxes(h, 2, 3), 0, 1)
    return h.reshape(bsz, seq, n_heads, v.shape[-1])


def mlstm_block(x, w_in, gate_b, head_g, w_out):
    bsz, seq, _ = x.shape
    p = jnp.einsum('bsd,dp->bsp', x, w_in)
    q, k, v, o, ig, fg = jnp.split(
        p, [MLSTM_QK, 2 * MLSTM_QK, 2 * MLSTM_QK + MLSTM_V, 2 * MLSTM_QK + 2 * MLSTM_V,
            2 * MLSTM_QK + 2 * MLSTM_V + MLSTM_HEADS], axis=-1)
    q = q.reshape(bsz, seq, MLSTM_HEADS, MLSTM_QK_DIM)
    k = k.reshape(bsz, seq, MLSTM_HEADS, MLSTM_QK_DIM) * (MLSTM_QK_DIM ** -0.5)
    v = v.reshape(bsz, seq, MLSTM_HEADS, MLSTM_V_DIM)
    li = (ig + gate_b[0]).astype(jnp.float32)
    lf = jax.nn.log_sigmoid((fg + gate_b[1]).astype(jnp.float32))
    h = mlstm_chunkwise(q, k, v, li, lf)
    h = h * lax.rsqrt(jnp.mean(h * h, axis=-1, keepdims=True) + EPS)
    h = h * head_g.astype(jnp.float32).reshape(MLSTM_HEADS, MLSTM_V_DIM)
    h = (h.reshape(bsz, seq, MLSTM_V) * jax.nn.sigmoid(o.astype(jnp.float32))).astype(x.dtype)
    return jnp.einsum('bsv,vd->bsd', h, w_out)


def swiglu(x, w_in, w_out):
    g, u = jnp.split(jnp.einsum('bsd,df->bsf', x, w_in), 2, axis=-1)
    return jnp.einsum('bsf,fd->bsd', jax.nn.silu(g) * u, w_out)


def setup_inputs(seed: int = 0) -> dict:
    key = jax.random.key(seed)
    ks = jax.random.split(key, 27)
    f32 = jnp.float32

    def nrm(k, shape, fan_in):
        return jax.random.normal(k, shape, f32) * (fan_in ** -0.5)

    def small(k, shape):
        return 0.02 * jax.random.normal(k, shape, f32)

    nA, nB, nC = N_CONV_LAYERS, N_LRU_LAYERS, N_MLSTM_LAYERS
    base = jax.random.uniform(ks[20], (nB, LRU_WIDTH), f32, minval=0.9, maxval=0.999) ** (1.0 / LRU_C)
    lam = jnp.log(base) - jnp.log1p(-base)
    i_bias = 0.1 * jax.random.normal(ks[24], (nC, 1, MLSTM_HEADS), f32)
    f_bias = jnp.linspace(3.0, 6.0, MLSTM_HEADS, dtype=f32)[None, None, :] + 0.1 * jax.random.normal(ks[25], (nC, 1, MLSTM_HEADS), f32)
    return {
        "x": jax.random.normal(ks[0], (BATCH, SEQ, D_MODEL), f32),
        "norm_g": 1.0 + small(ks[1], (DEPTH, 4, D_MODEL)),
        "ffn_w_in": nrm(ks[2], (DEPTH, D_MODEL, 2 * D_FF), D_MODEL),
        "ffn_w_out": nrm(ks[3], (DEPTH, D_FF, D_MODEL), D_FF),
        "conv_w_in": nrm(ks[4], (nA, D_MODEL, 2 * D_MODEL), D_MODEL),
        "conv_b_in": small(ks[5], (nA, 2 * D_MODEL)),
        "conv_dw": nrm(ks[6], (nA, CONV_WIDTH, D_MODEL), CONV_WIDTH),
        "conv_dw_b": small(ks[7], (nA, D_MODEL)),
        "conv_ln_g": 1.0 + small(ks[8], (nA, D_MODEL)),
        "conv_ln_b": small(ks[9], (nA, D_MODEL)),
        "conv_w_out": nrm(ks[10], (nA, D_MODEL, D_MODEL), D_MODEL),
        "conv_b_out": small(ks[11], (nA, D_MODEL)),
        "lru_w_in": nrm(ks[12], (nB, D_MODEL, 2 * LRU_WIDTH), D_MODEL),
        "lru_b_in": small(ks[13], (nB, 2 * LRU_WIDTH)),
        "lru_conv_w": nrm(ks[14], (nB, LRU_CONV_WIDTH, LRU_WIDTH), LRU_CONV_WIDTH),
        "lru_conv_b": small(ks[15], (nB, LRU_WIDTH)),
        "lru_gate_a_w": nrm(ks[16], (nB, LRU_HEADS, LRU_HEAD_DIM, LRU_HEAD_DIM), LRU_HEAD_DIM),
        "lru_gate_x_w": nrm(ks[17], (nB, LRU_HEADS, LRU_HEAD_DIM, LRU_HEAD_DIM), LRU_HEAD_DIM),
        "lru_gate_a_b": small(ks[18], (nB, LRU_WIDTH)),
        "lru_gate_x_b": small(ks[19], (nB, LRU_WIDTH)),
        "lru_lambda": lam,
        "lru_w_out": nrm(ks[21], (nB, LRU_WIDTH, D_MODEL), LRU_WIDTH),
        "lru_b_out": small(ks[22], (nB, D_MODEL)),
        "mlstm_w_in": nrm(ks[23], (nC, D_MODEL, MLSTM_PROJ), D_MODEL),
        "mlstm_gate_b": jnp.concatenate([i_bias, f_bias], axis=1),
        "mlstm_head_g": 1.0 + small(ks[26], (nC, MLSTM_V)),
        "mlstm_w_out": nrm(jax.random.fold_in(ks[26], 1), (nC, MLSTM_V, D_MODEL), MLSTM_V),
    }


def reference(x, norm_g, ffn_w_in, ffn_w_out, conv_w_in, conv_b_in, conv_dw, conv_dw_b,
              conv_ln_g, conv_ln_b, conv_w_out, conv_b_out, lru_w_in, lru_b_in, lru_conv_w,
              lru_conv_b, lru_gate_a_w, lru_gate_x_w, lru_gate_a_b, lru_gate_x_b, lru_lambda,
              lru_w_out, lru_b_out, mlstm_w_in, mlstm_gate_b, mlstm_head_g, mlstm_w_out):
    ia = ib = ic = 0
    for i in range(DEPTH):
        kind = i % N_MIXERS
        h = rms_norm(x, norm_g[i, 0])
        if kind == 0:
            y = conformer_conv_module(h, conv_w_in[ia], conv_b_in[ia], conv_dw[ia], conv_dw_b[ia],
                                      conv_ln_g[ia], conv_ln_b[ia], conv_w_out[ia], conv_b_out[ia])
            ia += 1
        elif kind == 1:
            y = rglru_block(h, lru_w_in[ib], lru_b_in[ib], lru_conv_w[ib], lru_conv_b[ib],
                            lru_gate_a_w[ib], lru_gate_x_w[ib], lru_gate_a_b[ib], lru_gate_x_b[ib],
                            lru_lambda[ib], lru_w_out[ib], lru_b_out[ib])
            ib += 1
        else:
            y = mlstm_block(h, mlstm_w_in[ic], mlstm_gate_b[ic], mlstm_head_g[ic], mlstm_w_out[ic])
            ic += 1
        x = x + rms_norm(y, norm_g[i, 1])
        y = swiglu(rms_norm(x, norm_g[i, 2]), ffn_w_in[i], ffn_w_out[i])
        x = x + rms_norm(y, norm_g[i, 3])
    return x
```

```python
import functools
import math

import jax
import jax.numpy as jnp
from jax import lax
from jax.experimental import pallas as pl
from jax.experimental.pallas import tpu as pltpu

EPS = 1e-6
LRU_C = 8.0
N_MIXERS = 3
MLSTM_CHUNK = 256
LANES = 128
SUBLANES = 8
VMEM_LIMIT_BYTES = 56 * 1024 * 1024

F32 = jnp.float32
BF16 = jnp.bfloat16


def _tile(n, pref, align=LANES):
    if n <= pref:
        return n
    t = (pref // align) * align
    while t >= align:
        if n % t == 0:
            return t
        t -= align
    return n


def _params(sem):
    return pltpu.CompilerParams(dimension_semantics=sem, vmem_limit_bytes=VMEM_LIMIT_BYTES)


def _sigmoid(z):
    return 1.0 / (1.0 + jnp.exp(-z))


def _softplus(z):
    return jnp.maximum(z, 0.0) + jnp.log1p(jnp.exp(-jnp.abs(z)))


def _gelu_tanh(z):
    c = math.sqrt(2.0 / math.pi)
    return 0.5 * z * (1.0 + jnp.tanh(c * (z + 0.044715 * (z * z * z))))


def _rms(y, g):
    ms = jnp.mean(y * y, axis=-1, keepdims=True)
    return y * lax.rsqrt(ms + EPS) * g


COL_CHUNK = 1024


def _col_chunks(d):
    step = min(COL_CHUNK, d)
    return [(c, min(step, d - c)) for c in range(0, d, step)]


def _inv_rms(load_chunk, d):
    ss = None
    for c, w in _col_chunks(d):
        y = load_chunk(c, w)
        part = jnp.sum(y * y, axis=-1, keepdims=True)
        ss = part if ss is None else ss + part
    return lax.rsqrt(ss * (1.0 / d) + EPS)


def _prenorm_to(xn_ref, x_ref, g_ref):
    d = x_ref.shape[1]
    inv = _inv_rms(lambda c, w: x_ref[:, c:c + w], d)
    for c, w in _col_chunks(d):
        xn_ref[:, c:c + w] = (x_ref[:, c:c + w] * inv * g_ref[:, c:c + w]).astype(xn_ref.dtype)


def _postnorm_residual_inplace(o_ref, xres_ref, g_ref, b_ref=None):
    d = o_ref.shape[1]

    def load(c, w):
        y = o_ref[:, c:c + w]
        return y if b_ref is None else y + b_ref[:, c:c + w]

    inv = _inv_rms(load, d)
    for c, w in _col_chunks(d):
        o_ref[:, c:c + w] = xres_ref[:, c:c + w] + load(c, w) * inv * g_ref[:, c:c + w]


def _accumulate_dot(o_ref, a, w_ref):
    d = o_ref.shape[1]
    for c, w in _col_chunks(d):
        o_ref[:, c:c + w] += jnp.dot(a, w_ref[:, c:c + w], preferred_element_type=F32)


def _norm_mm_kernel(x_ref, g_ref, w_ref, b_ref, o_ref, xn_ref):
    @pl.when(pl.program_id(1) == 0)
    def _():
        _prenorm_to(xn_ref, x_ref, g_ref)

    o_ref[...] = jnp.dot(xn_ref[...], w_ref[...], preferred_element_type=F32) + b_ref[...]


def _norm_mm_glu_kernel(x_ref, g_ref, wv_ref, wg_ref, bv_ref, bg_ref, o_ref, xn_ref):
    @pl.when(pl.program_id(1) == 0)
    def _():
        _prenorm_to(xn_ref, x_ref, g_ref)

    xn = xn_ref[...]
    val = jnp.dot(xn, wv_ref[...], preferred_element_type=F32) + bv_ref[...]
    gate = jnp.dot(xn, wg_ref[...], preferred_element_type=F32) + bg_ref[...]
    o_ref[...] = val * _sigmoid(gate)


def norm_matmul(x, g, w, b, *, tm_pref=512, tn_pref=1024):
    s, d = x.shape
    n = w.shape[1]
    tm, tn = _tile(s, tm_pref, SUBLANES), _tile(n, tn_pref)
    return pl.pallas_call(
        _norm_mm_kernel,
        grid=(s // tm, n // tn),
        in_specs=[
            pl.BlockSpec((tm, d), lambda i, j: (i, 0)),
            pl.BlockSpec((1, d), lambda i, j: (0, 0)),
            pl.BlockSpec((d, tn), lambda i, j: (0, j)),
            pl.BlockSpec((1, tn), lambda i, j: (0, j)),
        ],
        out_specs=pl.BlockSpec((tm, tn), lambda i, j: (i, j)),
        out_shape=jax.ShapeDtypeStruct((s, n), F32),
        scratch_shapes=[pltpu.VMEM((tm, d), BF16)],
        compiler_params=_params(("parallel", "arbitrary")),
        name="norm_matmul",
    )(x, g.reshape(1, d), w, b.reshape(1, n))


def norm_matmul_glu(x, g, w, b, *, tm_pref=512, tn_pref=512):
    s, d = x.shape
    n = w.shape[1] // 2
    tm, tn = _tile(s, tm_pref, SUBLANES), _tile(n, tn_pref)
    nj = n // tn
    b2 = b.reshape(1, 2 * n)
    return pl.pallas_call(
        _norm_mm_glu_kernel,
        grid=(s // tm, nj),
        in_specs=[
            pl.BlockSpec((tm, d), lambda i, j: (i, 0)),
            pl.BlockSpec((1, d), lambda i, j: (0, 0)),
            pl.BlockSpec((d, tn), lambda i, j: (0, j)),
            pl.BlockSpec((d, tn), lambda i, j: (0, j + nj)),
            pl.BlockSpec((1, tn), lambda i, j: (0, j)),
            pl.BlockSpec((1, tn), lambda i, j: (0, j + nj)),
        ],
        out_specs=pl.BlockSpec((tm, tn), lambda i, j: (i, j)),
        out_shape=jax.ShapeDtypeStruct((s, n), F32),
        scratch_shapes=[pltpu.VMEM((tm, d), BF16)],
        compiler_params=_params(("parallel", "arbitrary")),
        name="norm_matmul_glu",
    )(x, g.reshape(1, d), w, w, b2, b2)


def _mm_postnorm_kernel(a_ref, w_ref, b_ref, xres_ref, g_ref, o_ref):
    k = pl.program_id(1)

    @pl.when(k == 0)
    def _():
        o_ref[...] = jnp.zeros_like(o_ref)

    _accumulate_dot(o_ref, a_ref[...].astype(BF16), w_ref)

    @pl.when(k == pl.num_programs(1) - 1)
    def _():
        _postnorm_residual_inplace(o_ref, xres_ref, g_ref, b_ref)


def matmul_postnorm_residual(a, w, b, xres, g, *, tm_pref=512, tk_pref=512):
    s, kdim = a.shape
    d = w.shape[1]
    tm, tk = _tile(s, tm_pref, SUBLANES), _tile(kdim, tk_pref)
    return pl.pallas_call(
        _mm_postnorm_kernel,
        grid=(s // tm, kdim // tk),
        in_specs=[
            pl.BlockSpec((tm, tk), lambda i, k: (i, k)),
            pl.BlockSpec((tk, d), lambda i, k: (k, 0)),
            pl.BlockSpec((1, d), lambda i, k: (0, 0)),
            pl.BlockSpec((tm, d), lambda i, k: (i, 0)),
            pl.BlockSpec((1, d), lambda i, k: (0, 0)),
        ],
        out_specs=pl.BlockSpec((tm, d), lambda i, k: (i, 0)),
        out_shape=jax.ShapeDtypeStruct((s, d), F32),
        compiler_params=_params(("parallel", "arbitrary")),
        name="matmul_postnorm_residual",
    )(a, w, b.reshape(1, d), xres, g.reshape(1, d))


def _ffn_kernel(x_ref, gpre_ref, wg_ref, wu_ref, wo_ref, gpost_ref, o_ref, xn_ref):
    f = pl.program_id(1)

    @pl.when(f == 0)
    def _():
        _prenorm_to(xn_ref, x_ref, gpre_ref)
        o_ref[...] = jnp.zeros_like(o_ref)

    xn = xn_ref[...]
    gate = jnp.dot(xn, wg_ref[...], preferred_element_type=F32)
    up = jnp.dot(xn, wu_ref[...], preferred_element_type=F32)
    act = (gate * _sigmoid(gate) * up).astype(BF16)
    _accumulate_dot(o_ref, act, wo_ref)

    @pl.when(f == pl.num_programs(1) - 1)
    def _():
        _postnorm_residual_inplace(o_ref, x_ref, gpost_ref)


def ffn(x, g_pre, w_in, w_out, g_post, *, tm_pref=512, tf_pref=256):
    s, d = x.shape
    dff = w_out.shape[0]
    tm, tf = _tile(s, tm_pref, SUBLANES), _tile(dff, tf_pref)
    nf = dff // tf
    return pl.pallas_call(
        _ffn_kernel,
        grid=(s // tm, nf),
        in_specs=[
            pl.BlockSpec((tm, d), lambda i, f: (i, 0)),
            pl.BlockSpec((1, d), lambda i, f: (0, 0)),
            pl.BlockSpec((d, tf), lambda i, f: (0, f)),
            pl.BlockSpec((d, tf), lambda i, f: (0, f + nf)),
            pl.BlockSpec((tf, d), lambda i, f: (f, 0)),
            pl.BlockSpec((1, d), lambda i, f: (0, 0)),
        ],
        out_specs=pl.BlockSpec((tm, d), lambda i, f: (i, 0)),
        out_shape=jax.ShapeDtypeStruct((s, d), F32),
        scratch_shapes=[pltpu.VMEM((tm, d), BF16)],
        compiler_params=_params(("parallel", "arbitrary")),
        name="ffn",
    )(x, g_pre.reshape(1, d), w_in, w_in, w_out, g_post.reshape(1, d))


CONV_HALO = 32
CONV_ROWS = 32
CONV_COLS = 512


def _dwconv_kernel(hprev_ref, h_ref, dw_ref, dwb_ref, lng_ref, lnb_ref, o_ref, win_ref, y_ref):
    i = pl.program_id(0)
    t, d = h_ref.shape
    kw = dw_ref.shape[0]
    keep = (i > 0).astype(F32)
    win_ref[0:CONV_HALO, :] = hprev_ref[...] * keep
    win_ref[CONV_HALO:CONV_HALO + t, :] = h_ref[...]
    base = CONV_HALO - (kw - 1)
    rows = min(CONV_ROWS, t)
    cols = min(CONV_COLS, d)
    for r0 in range(0, t, rows):
        for c0 in range(0, d, cols):
            acc = jnp.broadcast_to(dwb_ref[:, c0:c0 + cols], (rows, cols))
            for k in range(kw):
                acc = acc + dw_ref[k:k + 1, c0:c0 + cols] * win_ref[base + r0 + k:base + r0 + k + rows, c0:c0 + cols]
            y_ref[r0:r0 + rows, c0:c0 + cols] = acc
    y = y_ref[...]
    mu = jnp.mean(y, axis=-1, keepdims=True)
    yc = y - mu
    var = jnp.mean(yc * yc, axis=-1, keepdims=True)
    z = yc * lax.rsqrt(var + EPS) * lng_ref[...] + lnb_ref[...]
    o_ref[...] = (z * _sigmoid(z)).astype(o_ref.dtype)


def dwconv_ln_silu(h, dw, dw_b, ln_g, ln_b, *, t_pref=128):
    s, d = h.shape
    kw = dw.shape[0]
    assert kw - 1 <= CONV_HALO
    t = _tile(s, t_pref, CONV_HALO)
    assert t % CONV_HALO == 0
    ratio = t // CONV_HALO
    return pl.pallas_call(
        _dwconv_kernel,
        grid=(s // t,),
        in_specs=[
            pl.BlockSpec((CONV_HALO, d), lambda i: (jnp.maximum(i * ratio - 1, 0), 0)),
            pl.BlockSpec((t, d), lambda i: (i, 0)),
            pl.BlockSpec((kw, d), lambda i: (0, 0)),
            pl.BlockSpec((1, d), lambda i: (0, 0)),
            pl.BlockSpec((1, d), lambda i: (0, 0)),
            pl.BlockSpec((1, d), lambda i: (0, 0)),
        ],
        out_specs=pl.BlockSpec((t, d), lambda i: (i, 0)),
        out_shape=jax.ShapeDtypeStruct((s, d), BF16),
        scratch_shapes=[pltpu.VMEM((CONV_HALO + t, d), F32), pltpu.VMEM((t, d), F32)],
        compiler_params=_params(("arbitrary",)),
        name="dwconv_ln_silu",
    )(h, h, dw, dw_b.reshape(1, d), ln_g.reshape(1, d), ln_b.reshape(1, d))


LRU_HALO = SUBLANES


def _lru_kernel(recprev_ref, rec_ref, gb_ref, cw_ref, cb_ref, wa_ref, wx_ref, ba_ref, bx_ref,
                lam_ref, y_ref, win_ref, a_ref, b_ref, h_ref, carry_ref):
    i = pl.program_id(1)
    t, c = rec_ref.shape
    kw = cw_ref.shape[0]

    @pl.when(i == 0)
    def _():
        carry_ref[...] = jnp.zeros_like(carry_ref)

    keep = (i > 0).astype(F32)
    win_ref[0:LRU_HALO, :] = recprev_ref[...] * keep
    win_ref[LRU_HALO:LRU_HALO + t, :] = rec_ref[...]
    base = LRU_HALO - (kw - 1)
    u = jnp.broadcast_to(cb_ref[...], (t, c))
    for k in range(kw):
        u = u + cw_ref[k:k + 1, :] * win_ref[base + k:base + k + t, :]

    ub = u.astype(BF16)
    r = _sigmoid(jnp.dot(ub, wa_ref[0], preferred_element_type=F32) + ba_ref[...])
    ig = _sigmoid(jnp.dot(ub, wx_ref[0], preferred_element_type=F32) + bx_ref[...])
    log_a = (-LRU_C) * r * _softplus(-lam_ref[...])
    a_ref[...] = jnp.exp(log_a)
    b_ref[...] = jnp.sqrt(1.0 - jnp.exp(2.0 * log_a)) * (ig * u)

    def step(tt, h):
        h = a_ref[pl.ds(tt, 1), :] * h + b_ref[pl.ds(tt, 1), :]
        h_ref[pl.ds(tt, 1), :] = h
        return h

    carry_ref[...] = lax.fori_loop(0, t, step, carry_ref[...], unroll=8)
    y_ref[...] = (h_ref[...] * _gelu_tanh(gb_ref[...])).astype(y_ref.dtype)


def lru_core(hin, conv_w, conv_b, wa, wx, ba, bx, lam, *, t_pref=256):
    s = hin.shape[0]
    r = hin.shape[1] // 2
    ng, c, _ = wa.shape
    kw = conv_w.shape[0]
    assert kw - 1 <= LRU_HALO and ng * c == r
    t = _tile(s, t_pref, LRU_HALO)
    ratio = t // LRU_HALO
    row = lambda v: v.reshape(1, r)
    vec_spec = pl.BlockSpec((1, c), lambda g, i: (0, g))
    return pl.pallas_call(
        _lru_kernel,
        grid=(ng, s // t),
        in_specs=[
            pl.BlockSpec((LRU_HALO, c), lambda g, i: (jnp.maximum(i * ratio - 1, 0), ng + g)),
            pl.BlockSpec((t, c), lambda g, i: (i, ng + g)),
            pl.BlockSpec((t, c), lambda g, i: (i, g)),
            pl.BlockSpec((kw, c), lambda g, i: (0, g)),
            vec_spec,
            pl.BlockSpec((1, c, c), lambda g, i: (g, 0, 0)),
            pl.BlockSpec((1, c, c), lambda g, i: (g, 0, 0)),
            vec_spec, vec_spec, vec_spec,
        ],
        out_specs=pl.BlockSpec((t, c), lambda g, i: (i, g)),
        out_shape=jax.ShapeDtypeStruct((s, r), BF16),
        scratch_shapes=[
            pltpu.VMEM((LRU_HALO + t, c), F32),
            pltpu.VMEM((t, c), F32),
            pltpu.VMEM((t, c), F32),
            pltpu.VMEM((t, c), F32),
            pltpu.VMEM((1, c), F32),
        ],
        compiler_params=_params(("parallel", "arbitrary")),
        name="lru_core",
    )(hin, hin, hin, conv_w, row(conv_b), wa, wx, row(ba), row(bx), row(lam))


def _block_diag_groups(w, heads_per_group):
    h, hd, _ = w.shape
    g = heads_per_group
    wg = w.reshape(h // g, g, hd, hd)
    eye = jnp.eye(g, dtype=w.dtype)
    return jnp.einsum('ngij,gh->ngihj', wg, eye).reshape(h // g, g * hd, g * hd)


def _heads_per_group(n_heads, head_dim):
    g = 1
    while (g * head_dim) % LANES != 0 and g < n_heads:
        g += 1
    assert n_heads % g == 0
    return g


def _mlstm_kernel(q_ref, k_ref, v_ref, o_ref, gates_ref, gbias_ref, hg_ref, out_ref,
                  c_ref, n_ref, m_ref, *, n_heads):
    ci = pl.program_id(0)
    L = q_ref.shape[0]
    dqk = q_ref.shape[1] // n_heads
    dv = v_ref.shape[1] // n_heads
    scale = dqk ** -0.5

    @pl.when(ci == 0)
    def _():
        c_ref[...] = jnp.zeros_like(c_ref)
        n_ref[...] = jnp.zeros_like(n_ref)
        m_ref[...] = jnp.zeros_like(m_ref)

    gz = gates_ref[...] + gbias_ref[...]
    lane = lax.broadcasted_iota(jnp.int32, gz.shape, 1)
    glog = jnp.where(lane < n_heads, gz, -_softplus(-gz))
    glog_t = glog.T

    rowi = lax.broadcasted_iota(jnp.int32, (L, L), 0)
    coli = lax.broadcasted_iota(jnp.int32, (L, L), 1)
    causal = coli <= rowi

    for h in range(n_heads):
        li_col = glog[:, h:h + 1]
        li_row = glog_t[h:h + 1, :]
        lf_col = glog[:, n_heads + h:n_heads + h + 1]
        lf_row = glog_t[n_heads + h:n_heads + h + 1, :]
        b_col = jnp.sum(jnp.where(causal, lf_row, 0.0), axis=1, keepdims=True)
        b_row = jnp.sum(jnp.where(rowi <= coli, lf_col, 0.0), axis=0, keepdims=True)
        m_st = m_ref[h][0:1, 0:1]
        d_mat = jnp.where(causal, b_col - b_row + li_row, -jnp.inf)
        m_inter = b_col + m_st
        m_out = jnp.maximum(m_inter, jnp.max(d_mat, axis=1, keepdims=True))
        w_inter = jnp.exp(m_inter - m_out)

        qh = q_ref[:, h * dqk:(h + 1) * dqk]
        kh = k_ref[:, h * dqk:(h + 1) * dqk] * scale
        vh = v_ref[:, h * dv:(h + 1) * dv]
        qb, kb, vb = qh.astype(BF16), kh.astype(BF16), vh.astype(BF16)

        qk = lax.dot_general(qb, kb, (((1,), (1,)), ((), ())), preferred_element_type=F32)
        s_mat = qk * jnp.exp(d_mat - m_out)
        c_st = c_ref[h]
        n_st = n_ref[h]
        num = w_inter * jnp.dot(qb, c_st.astype(BF16), preferred_element_type=F32) \
            + jnp.dot(s_mat.astype(BF16), vb, preferred_element_type=F32)
        den = w_inter * jnp.sum(qh * n_st, axis=1, keepdims=True) + jnp.sum(s_mat, axis=1, keepdims=True)
        hh = num / jnp.maximum(jnp.abs(den), jnp.exp(-m_out))

        b_last = b_col[L - 1:L, :]
        a_s = b_last - b_col + li_col
        m_new = jnp.maximum(b_last + m_st, jnp.max(a_s, axis=0, keepdims=True))
        w_old = jnp.exp(b_last + m_st - m_new)
        w_s = jnp.exp(a_s - m_new)
        kw_ = kh * w_s
        c_ref[h] = w_old * c_st + lax.dot_general(kw_.astype(BF16), vb, (((0,), (0,)), ((), ())),
                                                  preferred_element_type=F32)
        n_ref[h] = w_old * n_st + jnp.sum(kw_, axis=0, keepdims=True)
        m_ref[h] = jnp.broadcast_to(m_new, m_ref.shape[1:])

        hn = hh * lax.rsqrt(jnp.mean(hh * hh, axis=-1, keepdims=True) + EPS) * hg_ref[:, h * dv:(h + 1) * dv]
        out_ref[:, h * dv:(h + 1) * dv] = (hn * _sigmoid(o_ref[:, h * dv:(h + 1) * dv])).astype(out_ref.dtype)


def mlstm_core(p, gates, gate_bias_row, head_g, *, n_heads, qk, v):
    s = p.shape[0]
    L = _tile(s, MLSTM_CHUNK, SUBLANES)
    assert (2 * qk) % v == 0
    voff = (2 * qk) // v
    dqk, dv = qk // n_heads, v // n_heads
    return pl.pallas_call(
        functools.partial(_mlstm_kernel, n_heads=n_heads),
        grid=(s // L,),
        in_specs=[
            pl.BlockSpec((L, qk), lambda c: (c, 0)),
            pl.BlockSpec((L, qk), lambda c: (c, 1)),
            pl.BlockSpec((L, v), lambda c: (c, voff)),
            pl.BlockSpec((L, v), lambda c: (c, voff + 1)),
            pl.BlockSpec((L, LANES), lambda c: (c, 0)),
            pl.BlockSpec((1, LANES), lambda c: (0, 0)),
            pl.BlockSpec((1, v), lambda c: (0, 0)),
        ],
        out_specs=pl.BlockSpec((L, v), lambda c: (c, 0)),
        out_shape=jax.ShapeDtypeStruct((s, v), BF16),
        scratch_shapes=[
            pltpu.VMEM((n_heads, dqk, dv), F32),
            pltpu.VMEM((n_heads, 1, dqk), F32),
            pltpu.VMEM((n_heads, SUBLANES, LANES), F32),
        ],
        compiler_params=_params(("arbitrary",)),
        name="mlstm_core",
    )(p, p, p, p, gates, gate_bias_row, head_g.reshape(1, v))


def _conv_layer(x, g_pre, g_post, w_in, b_in, dw, dw_b, ln_g, ln_b, w_out, b_out):
    h = norm_matmul_glu(x, g_pre, w_in.astype(BF16), b_in)
    a = dwconv_ln_silu(h, dw, dw_b, ln_g, ln_b)
    return matmul_postnorm_residual(a, w_out.astype(BF16), b_out, x, g_post)


def _lru_layer(x, g_pre, g_post, w_in, b_in, conv_w, conv_b, gate_a_w, gate_x_w, gate_a_b, gate_x_b,
               lam, w_out, b_out):
    n_heads, head_dim, _ = gate_a_w.shape
    hpg = _heads_per_group(n_heads, head_dim)
    wa = _block_diag_groups(gate_a_w, hpg).astype(BF16)
    wx = _block_diag_groups(gate_x_w, hpg).astype(BF16)
    hin = norm_matmul(x, g_pre, w_in.astype(BF16), b_in)
    y = lru_core(hin, conv_w, conv_b, wa, wx, gate_a_b, gate_x_b, lam)
    return matmul_postnorm_residual(y, w_out.astype(BF16), b_out, x, g_post)


def _mlstm_layer(x, g_pre, g_post, w_in, gate_b, head_g, w_out):
    d = x.shape[1]
    n_heads = gate_b.shape[1]
    v = w_out.shape[0]
    proj = w_in.shape[1]
    qk = (proj - 2 * v - 2 * n_heads) // 2
    main = 2 * qk + 2 * v
    assert 2 * n_heads <= LANES
    w_main = w_in[:, :main].astype(BF16)
    w_gate = jnp.pad(w_in[:, main:], ((0, 0), (0, LANES - 2 * n_heads))).astype(BF16)
    p = norm_matmul(x, g_pre, w_main, jnp.zeros((main,), F32))
    gates = norm_matmul(x, g_pre, w_gate, jnp.zeros((LANES,), F32))
    gbias = jnp.pad(gate_b.reshape(1, 2 * n_heads), ((0, 0), (0, LANES - 2 * n_heads)))
    hcore = mlstm_core(p, gates, gbias, head_g, n_heads=n_heads, qk=qk, v=v)
    return matmul_postnorm_residual(hcore, w_out.astype(BF16), jnp.zeros((d,), F32), x, g_post)


def kernel(x, norm_g, ffn_w_in, ffn_w_out, conv_w_in, conv_b_in, conv_dw, conv_dw_b, conv_ln_g, conv_ln_b, conv_w_out, conv_b_out, lru_w_in, lru_b_in, lru_conv_w, lru_conv_b, lru_gate_a_w, lru_gate_x_w, lru_gate_a_b, lru_gate_x_b, lru_lambda, lru_w_out, lru_b_out, mlstm_w_in, mlstm_gate_b, mlstm_head_g, mlstm_w_out):
    bsz, seq, d = x.shape
    assert bsz == 1, "sequence mixers are written for a single sequence"
    xs = x.reshape(seq, d)
    ia = ib = ic = 0
    for i in range(norm_g.shape[0]):
        kind = i % N_MIXERS
        g = norm_g[i]
        if kind == 0:
            xs = _conv_layer(xs, g[0], g[1], conv_w_in[ia], conv_b_in[ia], conv_dw[ia], conv_dw_b[ia],
                             conv_ln_g[ia], conv_ln_b[ia], conv_w_out[ia], conv_b_out[ia])
            ia += 1
        elif kind == 1:
            xs = _lru_layer(xs, g[0], g[1], lru_w_in[ib], lru_b_in[ib], lru_conv_w[ib], lru_conv_b[ib],
                            lru_gate_a_w[ib], lru_gate_x_w[ib], lru_gate_a_b[ib], lru_gate_x_b[ib],
                            lru_lambda[ib], lru_w_out[ib], lru_b_out[ib])
            ib += 1
        else:
            xs = _mlstm_layer(xs, g[0], g[1], mlstm_w_in[ic], mlstm_gate_b[ic], mlstm_head_g[ic],
                              mlstm_w_out[ic])
            ic += 1
        xs = ffn(xs, g[2], ffn_w_in[i].astype(BF16), ffn_w_out[i].astype(BF16), g[3])
    return xs.reshape(bsz, seq, d)
```

```python
import functools
import math

import jax
import jax.numpy as jnp
from jax import lax
from jax.experimental import pallas as pl
from jax.experimental.pallas import tpu as pltpu

EPS = 1e-6
LRU_C = 8.0
N_MIXERS = 3
MLSTM_CHUNK = 256
LANES = 128
SUBLANES = 8
VMEM_LIMIT_BYTES = 56 * 1024 * 1024

F32 = jnp.float32
BF16 = jnp.bfloat16


def _tile(n, pref, align=LANES):
    if n <= pref:
        return n
    t = (pref // align) * align
    while t >= align:
        if n % t == 0:
            return t
        t -= align
    return n


def _wspec(w, layer, block, index_map):
    if w.ndim == 2:
        return pl.BlockSpec(block, index_map)
    return pl.BlockSpec((None,) + block, lambda *ids: (layer,) + index_map(*ids))


def _params(sem):
    return pltpu.CompilerParams(dimension_semantics=sem, vmem_limit_bytes=VMEM_LIMIT_BYTES)


def _sigmoid(z):
    return 1.0 / (1.0 + jnp.exp(-z))


def _softplus(z):
    return jnp.maximum(z, 0.0) + jnp.log1p(jnp.exp(-jnp.abs(z)))


def _gelu_tanh(z):
    c = math.sqrt(2.0 / math.pi)
    return 0.5 * z * (1.0 + jnp.tanh(c * (z + 0.044715 * (z * z * z))))


def _rms(y, g):
    ms = jnp.mean(y * y, axis=-1, keepdims=True)
    return y * lax.rsqrt(ms + EPS) * g


COL_CHUNK = 1024


def _col_chunks(d):
    step = min(COL_CHUNK, d)
    return [(c, min(step, d - c)) for c in range(0, d, step)]


def _inv_rms(load_chunk, d):
    ss = None
    for c, w in _col_chunks(d):
        y = load_chunk(c, w)
        part = jnp.sum(y * y, axis=-1, keepdims=True)
        ss = part if ss is None else ss + part
    return lax.rsqrt(ss * (1.0 / d) + EPS)


def _prenorm_to(xn_ref, x_ref, g_ref):
    d = x_ref.shape[1]
    inv = _inv_rms(lambda c, w: x_ref[:, c:c + w], d)
    for c, w in _col_chunks(d):
        xn_ref[:, c:c + w] = (x_ref[:, c:c + w] * inv * g_ref[:, c:c + w]).astype(xn_ref.dtype)


def _postnorm_residual_inplace(o_ref, xres_ref, g_ref, b_ref=None):
    d = o_ref.shape[1]

    def load(c, w):
        y = o_ref[:, c:c + w]
        return y if b_ref is None else y + b_ref[:, c:c + w]

    inv = _inv_rms(load, d)
    for c, w in _col_chunks(d):
        o_ref[:, c:c + w] = xres_ref[:, c:c + w] + load(c, w) * inv * g_ref[:, c:c + w]


def _accumulate_dot(o_ref, a, w_ref):
    d = o_ref.shape[1]
    for c, w in _col_chunks(d):
        o_ref[:, c:c + w] += jnp.dot(a, w_ref[:, c:c + w], preferred_element_type=F32)


def _norm_mm_kernel(x_ref, g_ref, w_ref, b_ref, o_ref, xn_ref):
    @pl.when(pl.program_id(1) == 0)
    def _():
        _prenorm_to(xn_ref, x_ref, g_ref)

    o_ref[...] = jnp.dot(xn_ref[...], w_ref[...], preferred_element_type=F32) + b_ref[...]


def _norm_mm_glu_kernel(x_ref, g_ref, wv_ref, wg_ref, bv_ref, bg_ref, o_ref, xn_ref):
    @pl.when(pl.program_id(1) == 0)
    def _():
        _prenorm_to(xn_ref, x_ref, g_ref)

    xn = xn_ref[...]
    val = jnp.dot(xn, wv_ref[...], preferred_element_type=F32) + bv_ref[...]
    gate = jnp.dot(xn, wg_ref[...], preferred_element_type=F32) + bg_ref[...]
    o_ref[...] = val * _sigmoid(gate)


def norm_matmul(x, g, w, b, *, layer=0, tm_pref=512, tn_pref=1024):
    s, d = x.shape
    n = w.shape[-1]
    tm, tn = _tile(s, tm_pref, SUBLANES), _tile(n, tn_pref)
    return pl.pallas_call(
        _norm_mm_kernel,
        grid=(s // tm, n // tn),
        in_specs=[
            pl.BlockSpec((tm, d), lambda i, j: (i, 0)),
            pl.BlockSpec((1, d), lambda i, j: (0, 0)),
            _wspec(w, layer, (d, tn), lambda i, j: (0, j)),
            pl.BlockSpec((1, tn), lambda i, j: (0, j)),
        ],
        out_specs=pl.BlockSpec((tm, tn), lambda i, j: (i, j)),
        out_shape=jax.ShapeDtypeStruct((s, n), F32),
        scratch_shapes=[pltpu.VMEM((tm, d), BF16)],
        compiler_params=_params(("parallel", "arbitrary")),
        name="norm_matmul",
    )(x, g.reshape(1, d), w, b.reshape(1, n))


def norm_matmul_glu(x, g, w, b, *, layer=0, tm_pref=512, tn_pref=512):
    s, d = x.shape
    n = w.shape[-1] // 2
    tm, tn = _tile(s, tm_pref, SUBLANES), _tile(n, tn_pref)
    nj = n // tn
    b2 = b.reshape(1, 2 * n)
    return pl.pallas_call(
        _norm_mm_glu_kernel,
        grid=(s // tm, nj),
        in_specs=[
            pl.BlockSpec((tm, d), lambda i, j: (i, 0)),
            pl.BlockSpec((1, d), lambda i, j: (0, 0)),
            _wspec(w, layer, (d, tn), lambda i, j: (0, j)),
            _wspec(w, layer, (d, tn), lambda i, j: (0, j + nj)),
            pl.BlockSpec((1, tn), lambda i, j: (0, j)),
            pl.BlockSpec((1, tn), lambda i, j: (0, j + nj)),
        ],
        out_specs=pl.BlockSpec((tm, tn), lambda i, j: (i, j)),
        out_shape=jax.ShapeDtypeStruct((s, n), F32),
        scratch_shapes=[pltpu.VMEM((tm, d), BF16)],
        compiler_params=_params(("parallel", "arbitrary")),
        name="norm_matmul_glu",
    )(x, g.reshape(1, d), w, w, b2, b2)


def _mm_postnorm_kernel(a_ref, w_ref, b_ref, xres_ref, g_ref, o_ref):
    k = pl.program_id(1)

    @pl.when(k == 0)
    def _():
        o_ref[...] = jnp.zeros_like(o_ref)

    _accumulate_dot(o_ref, a_ref[...].astype(BF16), w_ref)

    @pl.when(k == pl.num_programs(1) - 1)
    def _():
        _postnorm_residual_inplace(o_ref, xres_ref, g_ref, b_ref)


def matmul_postnorm_residual(a, w, b, xres, g, *, layer=0, tm_pref=512, tk_pref=1024):
    s, kdim = a.shape
    d = w.shape[-1]
    tm, tk = _tile(s, tm_pref, SUBLANES), _tile(kdim, tk_pref)
    return pl.pallas_call(
        _mm_postnorm_kernel,
        grid=(s // tm, kdim // tk),
        in_specs=[
            pl.BlockSpec((tm, tk), lambda i, k: (i, k)),
            _wspec(w, layer, (tk, d), lambda i, k: (k, 0)),
            pl.BlockSpec((1, d), lambda i, k: (0, 0)),
            pl.BlockSpec((tm, d), lambda i, k: (i, 0)),
            pl.BlockSpec((1, d), lambda i, k: (0, 0)),
        ],
        out_specs=pl.BlockSpec((tm, d), lambda i, k: (i, 0)),
        out_shape=jax.ShapeDtypeStruct((s, d), F32),
        compiler_params=_params(("parallel", "arbitrary")),
        name="matmul_postnorm_residual",
    )(a, w, b.reshape(1, d), xres, g.reshape(1, d))


def _ffn_kernel(x_ref, gpre_ref, wg_ref, wu_ref, wo_ref, gpost_ref, o_ref, xn_ref):
    f = pl.program_id(1)

    @pl.when(f == 0)
    def _():
        _prenorm_to(xn_ref, x_ref, gpre_ref)
        o_ref[...] = jnp.zeros_like(o_ref)

    xn = xn_ref[...]
    gate = jnp.dot(xn, wg_ref[...], preferred_element_type=F32)
    up = jnp.dot(xn, wu_ref[...], preferred_element_type=F32)
    act = (gate * _sigmoid(gate) * up).astype(BF16)
    _accumulate_dot(o_ref, act, wo_ref)

    @pl.when(f == pl.num_programs(1) - 1)
    def _():
        _postnorm_residual_inplace(o_ref, x_ref, gpost_ref)


def ffn(x, g_pre, w_in, w_out, g_post, *, layer=0, tm_pref=512, tf_pref=256):
    s, d = x.shape
    dff = w_out.shape[-2]
    tm, tf = _tile(s, tm_pref, SUBLANES), _tile(dff, tf_pref)
    nf = dff // tf
    return pl.pallas_call(
        _ffn_kernel,
        grid=(s // tm, nf),
        in_specs=[
            pl.BlockSpec((tm, d), lambda i, f: (i, 0)),
            pl.BlockSpec((1, d), lambda i, f: (0, 0)),
            _wspec(w_in, layer, (d, tf), lambda i, f: (0, f)),
            _wspec(w_in, layer, (d, tf), lambda i, f: (0, f + nf)),
            _wspec(w_out, layer, (tf, d), lambda i, f: (f, 0)),
            pl.BlockSpec((1, d), lambda i, f: (0, 0)),
        ],
        out_specs=pl.BlockSpec((tm, d), lambda i, f: (i, 0)),
        out_shape=jax.ShapeDtypeStruct((s, d), F32),
        scratch_shapes=[pltpu.VMEM((tm, d), BF16)],
        compiler_params=_params(("parallel", "arbitrary")),
        name="ffn",
    )(x, g_pre.reshape(1, d), w_in, w_in, w_out, g_post.reshape(1, d))


CONV_HALO = 32
CONV_ROWS = 32


def _dwconv_kernel(hprev_ref, h_ref, dw_ref, dwb_ref, lng_ref, lnb_ref, o_ref, win_ref, y_ref):
    i = pl.program_id(0)
    t, d = h_ref.shape
    kw = dw_ref.shape[0]
    keep = (i > 0).astype(F32)
    lane_tiles = [slice(c, c + LANES) for c in range(0, d, LANES)]
    for c, sl in enumerate(lane_tiles):
        win_ref[c, 0:CONV_HALO, :] = hprev_ref[:, sl] * keep
        win_ref[c, CONV_HALO:CONV_HALO + t, :] = h_ref[:, sl]
    base = CONV_HALO - (kw - 1)
    rows = min(CONV_ROWS, t)
    for c, sl in enumerate(lane_tiles):
        for r0 in range(0, t, rows):
            acc = jnp.broadcast_to(dwb_ref[:, sl], (rows, LANES))
            for k in range(kw):
                acc = acc + dw_ref[k:k + 1, sl] * win_ref[c, base + r0 + k:base + r0 + k + rows, :]
            y_ref[r0:r0 + rows, sl] = acc
    y = y_ref[...]
    mu = jnp.mean(y, axis=-1, keepdims=True)
    yc = y - mu
    var = jnp.mean(yc * yc, axis=-1, keepdims=True)
    z = yc * lax.rsqrt(var + EPS) * lng_ref[...] + lnb_ref[...]
    o_ref[...] = (z * _sigmoid(z)).astype(o_ref.dtype)


def dwconv_ln_silu(h, dw, dw_b, ln_g, ln_b, *, t_pref=128):
    s, d = h.shape
    kw = dw.shape[0]
    assert kw - 1 <= CONV_HALO and d % LANES == 0
    t = _tile(s, t_pref, CONV_HALO)
    assert t % CONV_HALO == 0
    ratio = t // CONV_HALO
    return pl.pallas_call(
        _dwconv_kernel,
        grid=(s // t,),
        in_specs=[
            pl.BlockSpec((CONV_HALO, d), lambda i: (jnp.maximum(i * ratio - 1, 0), 0)),
            pl.BlockSpec((t, d), lambda i: (i, 0)),
            pl.BlockSpec((kw, d), lambda i: (0, 0)),
            pl.BlockSpec((1, d), lambda i: (0, 0)),
            pl.BlockSpec((1, d), lambda i: (0, 0)),
            pl.BlockSpec((1, d), lambda i: (0, 0)),
        ],
        out_specs=pl.BlockSpec((t, d), lambda i: (i, 0)),
        out_shape=jax.ShapeDtypeStruct((s, d), BF16),
        scratch_shapes=[pltpu.VMEM((d // LANES, CONV_HALO + t, LANES), F32), pltpu.VMEM((t, d), F32)],
        compiler_params=_params(("arbitrary",)),
        name="dwconv_ln_silu",
    )(h, h, dw, dw_b.reshape(1, d), ln_g.reshape(1, d), ln_b.reshape(1, d))


LRU_HALO = SUBLANES


def _lru_kernel(recprev_ref, rec_ref, gb_ref, cw_ref, cb_ref, wa_ref, wx_ref, ba_ref, bx_ref,
                lam_ref, y_ref, win_ref, a_ref, b_ref, h_ref, carry_ref):
    i = pl.program_id(1)
    t, c = rec_ref.shape
    kw = cw_ref.shape[0]

    @pl.when(i == 0)
    def _():
        carry_ref[...] = jnp.zeros_like(carry_ref)

    keep = (i > 0).astype(F32)
    win_ref[0:LRU_HALO, :] = recprev_ref[...] * keep
    win_ref[LRU_HALO:LRU_HALO + t, :] = rec_ref[...]
    base = LRU_HALO - (kw - 1)
    u = jnp.broadcast_to(cb_ref[...], (t, c))
    for k in range(kw):
        u = u + cw_ref[k:k + 1, :] * win_ref[base + k:base + k + t, :]

    ub = u.astype(BF16)
    r = _sigmoid(jnp.dot(ub, wa_ref[0], preferred_element_type=F32) + ba_ref[...])
    ig = _sigmoid(jnp.dot(ub, wx_ref[0], preferred_element_type=F32) + bx_ref[...])
    log_a = (-LRU_C) * r * _softplus(-lam_ref[...])
    a_ref[...] = jnp.exp(log_a)
    b_ref[...] = jnp.sqrt(1.0 - jnp.exp(2.0 * log_a)) * (ig * u)

    def step(tt, h):
        h = a_ref[pl.ds(tt, 1), :] * h + b_ref[pl.ds(tt, 1), :]
        h_ref[pl.ds(tt, 1), :] = h
        return h

    carry_ref[...] = lax.fori_loop(0, t, step, carry_ref[...], unroll=8)
    y_ref[...] = (h_ref[...] * _gelu_tanh(gb_ref[...])).astype(y_ref.dtype)


def lru_core(hin, conv_w, conv_b, wa, wx, ba, bx, lam, *, t_pref=256):
    s = hin.shape[0]
    r = hin.shape[1] // 2
    ng, c, _ = wa.shape
    kw = conv_w.shape[0]
    assert kw - 1 <= LRU_HALO and ng * c == r
    t = _tile(s, t_pref, LRU_HALO)
    ratio = t // LRU_HALO
    row = lambda v: v.reshape(1, r)
    vec_spec = pl.BlockSpec((1, c), lambda g, i: (0, g))
    return pl.pallas_call(
        _lru_kernel,
        grid=(ng, s // t),
        in_specs=[
            pl.BlockSpec((LRU_HALO, c), lambda g, i: (jnp.maximum(i * ratio - 1, 0), ng + g)),
            pl.BlockSpec((t, c), lambda g, i: (i, ng + g)),
            pl.BlockSpec((t, c), lambda g, i: (i, g)),
            pl.BlockSpec((kw, c), lambda g, i: (0, g)),
            vec_spec,
            pl.BlockSpec((1, c, c), lambda g, i: (g, 0, 0)),
            pl.BlockSpec((1, c, c), lambda g, i: (g, 0, 0)),
            vec_spec, vec_spec, vec_spec,
        ],
        out_specs=pl.BlockSpec((t, c), lambda g, i: (i, g)),
        out_shape=jax.ShapeDtypeStruct((s, r), BF16),
        scratch_shapes=[
            pltpu.VMEM((LRU_HALO + t, c), F32),
            pltpu.VMEM((t, c), F32),
            pltpu.VMEM((t, c), F32),
            pltpu.VMEM((t, c), F32),
            pltpu.VMEM((1, c), F32),
        ],
        compiler_params=_params(("parallel", "arbitrary")),
        name="lru_core",
    )(hin, hin, hin, conv_w, row(conv_b), wa, wx, row(ba), row(bx), row(lam))


def _block_diag_groups(w, heads_per_group):
    h, hd, _ = w.shape
    g = heads_per_group
    wg = w.reshape(h // g, g, hd, hd)
    eye = jnp.eye(g, dtype=w.dtype)
    return jnp.einsum('ngij,gh->ngihj', wg, eye).reshape(h // g, g * hd, g * hd)


def _heads_per_group(n_heads, head_dim):
    g = 1
    while (g * head_dim) % LANES != 0 and g < n_heads:
        g += 1
    assert n_heads % g == 0
    return g


def _mlstm_kernel(q_ref, k_ref, v_ref, o_ref, gates_ref, gbias_ref, hg_ref, out_ref,
                  c_ref, n_ref, m_ref, *, n_heads):
    ci = pl.program_id(0)
    L = q_ref.shape[0]
    dqk = q_ref.shape[1] // n_heads
    dv = v_ref.shape[1] // n_heads
    scale = dqk ** -0.5

    @pl.when(ci == 0)
    def _():
        c_ref[...] = jnp.zeros_like(c_ref)
        n_ref[...] = jnp.zeros_like(n_ref)
        m_ref[...] = jnp.zeros_like(m_ref)

    gz = gates_ref[...] + gbias_ref[...]
    lane = lax.broadcasted_iota(jnp.int32, gz.shape, 1)
    glog = jnp.where(lane < n_heads, gz, -_softplus(-gz))
    glog_t = glog.T

    rowi = lax.broadcasted_iota(jnp.int32, (L, L), 0)
    coli = lax.broadcasted_iota(jnp.int32, (L, L), 1)
    causal = coli <= rowi

    for h in range(n_heads):
        li_col = glog[:, h:h + 1]
        li_row = glog_t[h:h + 1, :]
        lf_col = glog[:, n_heads + h:n_heads + h + 1]
        lf_row = glog_t[n_heads + h:n_heads + h + 1, :]
        b_col = jnp.sum(jnp.where(causal, lf_row, 0.0), axis=1, keepdims=True)
        b_row = jnp.sum(jnp.where(rowi <= coli, lf_col, 0.0), axis=0, keepdims=True)
        m_st = m_ref[h][0:1, 0:1]
        d_mat = jnp.where(causal, b_col - b_row + li_row, -jnp.inf)
        m_inter = b_col + m_st
        m_out = jnp.maximum(m_inter, jnp.max(d_mat, axis=1, keepdims=True))
        w_inter = jnp.exp(m_inter - m_out)

        qh = q_ref[:, h * dqk:(h + 1) * dqk]
        kh = k_ref[:, h * dqk:(h + 1) * dqk] * scale
        vh = v_ref[:, h * dv:(h + 1) * dv]
        qb, kb, vb = qh.astype(BF16), kh.astype(BF16), vh.astype(BF16)

        qk = lax.dot_general(qb, kb, (((1,), (1,)), ((), ())), preferred_element_type=F32)
        s_mat = qk * jnp.exp(d_mat - m_out)
        c_st = c_ref[h]
        n_st = n_ref[h]
        num = w_inter * jnp.dot(qb, c_st.astype(BF16), preferred_element_type=F32) \
            + jnp.dot(s_mat.astype(BF16), vb, preferred_element_type=F32)
        den = w_inter * jnp.sum(qh * n_st, axis=1, keepdims=True) + jnp.sum(s_mat, axis=1, keepdims=True)
        hh = num / jnp.maximum(jnp.abs(den), jnp.exp(-m_out))

        b_last = b_col[L - 1:L, :]
        a_s = b_last - b_col + li_col
        m_new = jnp.maximum(b_last + m_st, jnp.max(a_s, axis=0, keepdims=True))
        w_old = jnp.exp(b_last + m_st - m_new)
        w_s = jnp.exp(a_s - m_new)
        kw_ = kh * w_s
        c_ref[h] = w_old * c_st + lax.dot_general(kw_.astype(BF16), vb, (((0,), (0,)), ((), ())),
                                                  preferred_element_type=F32)
        n_ref[h] = w_old * n_st + jnp.sum(kw_, axis=0, keepdims=True)
        m_ref[h] = jnp.broadcast_to(m_new, m_ref.shape[1:])

        hn = hh * lax.rsqrt(jnp.mean(hh * hh, axis=-1, keepdims=True) + EPS) * hg_ref[:, h * dv:(h + 1) * dv]
        out_ref[:, h * dv:(h + 1) * dv] = (hn * _sigmoid(o_ref[:, h * dv:(h + 1) * dv])).astype(out_ref.dtype)


def mlstm_core(p, gates, gate_bias_row, head_g, *, n_heads, qk, v):
    s = p.shape[0]
    L = _tile(s, MLSTM_CHUNK, SUBLANES)
    assert (2 * qk) % v == 0
    voff = (2 * qk) // v
    dqk, dv = qk // n_heads, v // n_heads
    return pl.pallas_call(
        functools.partial(_mlstm_kernel, n_heads=n_heads),
        grid=(s // L,),
        in_specs=[
            pl.BlockSpec((L, qk), lambda c: (c, 0)),
            pl.BlockSpec((L, qk), lambda c: (c, 1)),
            pl.BlockSpec((L, v), lambda c: (c, voff)),
            pl.BlockSpec((L, v), lambda c: (c, voff + 1)),
            pl.BlockSpec((L, LANES), lambda c: (c, 0)),
            pl.BlockSpec((1, LANES), lambda c: (0, 0)),
            pl.BlockSpec((1, v), lambda c: (0, 0)),
        ],
        out_specs=pl.BlockSpec((L, v), lambda c: (c, 0)),
        out_shape=jax.ShapeDtypeStruct((s, v), BF16),
        scratch_shapes=[
            pltpu.VMEM((n_heads, dqk, dv), F32),
            pltpu.VMEM((n_heads, 1, dqk), F32),
            pltpu.VMEM((n_heads, SUBLANES, LANES), F32),
        ],
        compiler_params=_params(("arbitrary",)),
        name="mlstm_core",
    )(p, p, p, p, gates, gate_bias_row, head_g.reshape(1, v))


def _conv_layer(x, g_pre, g_post, layer, w_in, b_in, dw, dw_b, ln_g, ln_b, w_out, b_out):
    h = norm_matmul_glu(x, g_pre, w_in, b_in, layer=layer)
    a = dwconv_ln_silu(h, dw, dw_b, ln_g, ln_b)
    return matmul_postnorm_residual(a, w_out, b_out, x, g_post, layer=layer)


def _lru_layer(x, g_pre, g_post, layer, w_in, b_in, conv_w, conv_b, gate_a_w, gate_x_w, gate_a_b, gate_x_b,
               lam, w_out, b_out):
    n_heads, head_dim, _ = gate_a_w.shape
    hpg = _heads_per_group(n_heads, head_dim)
    wa = _block_diag_groups(gate_a_w, hpg).astype(BF16)
    wx = _block_diag_groups(gate_x_w, hpg).astype(BF16)
    hin = norm_matmul(x, g_pre, w_in, b_in, layer=layer)
    y = lru_core(hin, conv_w, conv_b, wa, wx, gate_a_b, gate_x_b, lam)
    return matmul_postnorm_residual(y, w_out, b_out, x, g_post, layer=layer)


def _mlstm_layer(x, g_pre, g_post, w_in, gate_b, head_g, w_out):
    d = x.shape[1]
    n_heads = gate_b.shape[1]
    v = w_out.shape[0]
    proj = w_in.shape[1]
    qk = (proj - 2 * v - 2 * n_heads) // 2
    main = 2 * qk + 2 * v
    assert 2 * n_heads <= LANES
    w_main = w_in[:, :main].astype(BF16)
    w_gate = jnp.pad(w_in[:, main:], ((0, 0), (0, LANES - 2 * n_heads))).astype(BF16)
    p = norm_matmul(x, g_pre, w_main, jnp.zeros((main,), F32))
    gates = norm_matmul(x, g_pre, w_gate, jnp.zeros((LANES,), F32))
    gbias = jnp.pad(gate_b.reshape(1, 2 * n_heads), ((0, 0), (0, LANES - 2 * n_heads)))
    hcore = mlstm_core(p, gates, gbias, head_g, n_heads=n_heads, qk=qk, v=v)
    return matmul_postnorm_residual(hcore, w_out.astype(BF16), jnp.zeros((d,), F32), x, g_post)


def kernel(x, norm_g, ffn_w_in, ffn_w_out, conv_w_in, conv_b_in, conv_dw, conv_dw_b, conv_ln_g, conv_ln_b, conv_w_out, conv_b_out, lru_w_in, lru_b_in, lru_conv_w, lru_conv_b, lru_gate_a_w, lru_gate_x_w, lru_gate_a_b, lru_gate_x_b, lru_lambda, lru_w_out, lru_b_out, mlstm_w_in, mlstm_gate_b, mlstm_head_g, mlstm_w_out):
    bsz, seq, d = x.shape
    assert bsz == 1, "sequence mixers are written for a single sequence"
    xs = x.reshape(seq, d)
    ffn_in, ffn_out = ffn_w_in.astype(BF16), ffn_w_out.astype(BF16)
    conv_in, conv_out = conv_w_in.astype(BF16), conv_w_out.astype(BF16)
    lru_in, lru_out = lru_w_in.astype(BF16), lru_w_out.astype(BF16)
    ia = ib = ic = 0
    for i in range(norm_g.shape[0]):
        kind = i % N_MIXERS
        g = norm_g[i]
        if kind == 0:
            xs = _conv_layer(xs, g[0], g[1], ia, conv_in, conv_b_in[ia], conv_dw[ia], conv_dw_b[ia],
                             conv_ln_g[ia], conv_ln_b[ia], conv_out, conv_b_out[ia])
            ia += 1
        elif kind == 1:
            xs = _lru_layer(xs, g[0], g[1], ib, lru_in, lru_b_in[ib], lru_conv_w[ib], lru_conv_b[ib],
                            lru_gate_a_w[ib], lru_gate_x_w[ib], lru_gate_a_b[ib], lru_gate_x_b[ib],
                            lru_lambda[ib], lru_out, lru_b_out[ib])
            ib += 1
        else:
            xs = _mlstm_layer(xs, g[0], g[1], mlstm_w_in[ic], mlstm_gate_b[ic], mlstm_head_g[ic],
                              mlstm_w_out[ic])
            ic += 1
        xs = ffn(xs, g[2], ffn_in, ffn_out, g[3], layer=i)
    return xs.reshape(bsz, seq, d)
```

```python
import functools
import math

import jax
import jax.numpy as jnp
from jax import lax
from jax.experimental import pallas as pl
from jax.experimental.pallas import tpu as pltpu

EPS = 1e-6
LRU_C = 8.0
N_MIXERS = 3
MLSTM_CHUNK = 256
LANES = 128
SUBLANES = 8
VMEM_LIMIT_BYTES = 56 * 1024 * 1024

F32 = jnp.float32
BF16 = jnp.bfloat16


def _tile(n, pref, align=LANES):
    if n <= pref:
        return n
    t = (pref // align) * align
    while t >= align:
        if n % t == 0:
            return t
        t -= align
    return n


def _wspec(w, layer, block, index_map):
    if w.ndim == 2:
        return pl.BlockSpec(block, index_map)
    return pl.BlockSpec((None,) + block, lambda *ids: (layer,) + index_map(*ids))


def _params(sem):
    return pltpu.CompilerParams(dimension_semantics=sem, vmem_limit_bytes=VMEM_LIMIT_BYTES)


def _sigmoid(z):
    return 1.0 / (1.0 + jnp.exp(-z))


def _softplus(z):
    return jnp.maximum(z, 0.0) + jnp.log1p(jnp.exp(-jnp.abs(z)))


def _gelu_tanh(z):
    c = math.sqrt(2.0 / math.pi)
    return 0.5 * z * (1.0 + jnp.tanh(c * (z + 0.044715 * (z * z * z))))


def _rms(y, g):
    ms = jnp.mean(y * y, axis=-1, keepdims=True)
    return y * lax.rsqrt(ms + EPS) * g


COL_CHUNK = 1024


def _col_chunks(d):
    step = min(COL_CHUNK, d)
    return [(c, min(step, d - c)) for c in range(0, d, step)]


def _inv_rms(load_chunk, d):
    ss = None
    for c, w in _col_chunks(d):
        y = load_chunk(c, w)
        part = jnp.sum(y * y, axis=-1, keepdims=True)
        ss = part if ss is None else ss + part
    return lax.rsqrt(ss * (1.0 / d) + EPS)


def _prenorm_to(xn_ref, x_ref, g_ref):
    d = x_ref.shape[1]
    inv = _inv_rms(lambda c, w: x_ref[:, c:c + w], d)
    for c, w in _col_chunks(d):
        xn_ref[:, c:c + w] = (x_ref[:, c:c + w] * inv * g_ref[:, c:c + w]).astype(xn_ref.dtype)


def _postnorm_residual_inplace(o_ref, xres_ref, g_ref, b_ref=None):
    d = o_ref.shape[1]

    def load(c, w):
        y = o_ref[:, c:c + w]
        return y if b_ref is None else y + b_ref[:, c:c + w]

    inv = _inv_rms(load, d)
    for c, w in _col_chunks(d):
        o_ref[:, c:c + w] = xres_ref[:, c:c + w] + load(c, w) * inv * g_ref[:, c:c + w]


def _assign_dot(o_ref, a, w_ref):
    d = o_ref.shape[1]
    for c, w in _col_chunks(d):
        o_ref[:, c:c + w] = jnp.dot(a, w_ref[:, c:c + w], preferred_element_type=F32)


def _accumulate_dot(o_ref, a, w_ref):
    d = o_ref.shape[1]
    for c, w in _col_chunks(d):
        o_ref[:, c:c + w] += jnp.dot(a, w_ref[:, c:c + w], preferred_element_type=F32)


def _norm_mm_kernel(x_ref, g_ref, w_ref, b_ref, o_ref, xn_ref):
    @pl.when(pl.program_id(1) == 0)
    def _():
        _prenorm_to(xn_ref, x_ref, g_ref)

    o_ref[...] = jnp.dot(xn_ref[...], w_ref[...], preferred_element_type=F32) + b_ref[...]


def norm_matmul(x, g, w, b, *, layer=0, tm_pref=512, tn_pref=1024):
    s, d = x.shape
    n = w.shape[-1]
    tm, tn = _tile(s, tm_pref, SUBLANES), _tile(n, tn_pref)
    return pl.pallas_call(
        _norm_mm_kernel,
        grid=(s // tm, n // tn),
        in_specs=[
            pl.BlockSpec((tm, d), lambda i, j: (i, 0)),
            pl.BlockSpec((1, d), lambda i, j: (0, 0)),
            _wspec(w, layer, (d, tn), lambda i, j: (0, j)),
            pl.BlockSpec((1, tn), lambda i, j: (0, j)),
        ],
        out_specs=pl.BlockSpec((tm, tn), lambda i, j: (i, j)),
        out_shape=jax.ShapeDtypeStruct((s, n), F32),
        scratch_shapes=[pltpu.VMEM((tm, d), BF16)],
        compiler_params=_params(("parallel", "arbitrary")),
        name="norm_matmul",
    )(x, g.reshape(1, d), w, b.reshape(1, n))


def _mm_postnorm_kernel(a_ref, w_ref, b_ref, xres_ref, g_ref, o_ref):
    k = pl.program_id(1)

    @pl.when(k == 0)
    def _():
        _assign_dot(o_ref, a_ref[...].astype(BF16), w_ref)

    @pl.when(k > 0)
    def _():
        _accumulate_dot(o_ref, a_ref[...].astype(BF16), w_ref)

    @pl.when(k == pl.num_programs(1) - 1)
    def _():
        _postnorm_residual_inplace(o_ref, xres_ref, g_ref, b_ref)


def matmul_postnorm_residual(a, w, b, xres, g, *, layer=0, tm_pref=512, tk_pref=1024):
    s, kdim = a.shape
    d = w.shape[-1]
    tm, tk = _tile(s, tm_pref, SUBLANES), _tile(kdim, tk_pref)
    return pl.pallas_call(
        _mm_postnorm_kernel,
        grid=(s // tm, kdim // tk),
        in_specs=[
            pl.BlockSpec((tm, tk), lambda i, k: (i, k)),
            _wspec(w, layer, (tk, d), lambda i, k: (k, 0)),
            pl.BlockSpec((1, d), lambda i, k: (0, 0)),
            pl.BlockSpec((tm, d), lambda i, k: (i, 0)),
            pl.BlockSpec((1, d), lambda i, k: (0, 0)),
        ],
        out_specs=pl.BlockSpec((tm, d), lambda i, k: (i, 0)),
        out_shape=jax.ShapeDtypeStruct((s, d), F32),
        compiler_params=_params(("parallel", "arbitrary")),
        name="matmul_postnorm_residual",
    )(a, w, b.reshape(1, d), xres, g.reshape(1, d))


def _ffn_kernel(x_ref, gpre_ref, wg_ref, wu_ref, wo_ref, gpost_ref, o_ref, xn_ref, act_ref):
    f = pl.program_id(1)
    nf = pl.num_programs(1) - 1

    def up_proj():
        xn = xn_ref[...]
        gate = jnp.dot(xn, wg_ref[...], preferred_element_type=F32)
        up = jnp.dot(xn, wu_ref[...], preferred_element_type=F32)
        return (gate * _sigmoid(gate) * up).astype(BF16)

    @pl.when(f == 0)
    def _():
        _prenorm_to(xn_ref, x_ref, gpre_ref)
        o_ref[...] = jnp.zeros_like(o_ref)
        act_ref[...] = up_proj()

    @pl.when(jnp.logical_and(f > 0, f < nf))
    def _():
        prev = act_ref[...]
        _accumulate_dot(o_ref, prev, wo_ref)
        act_ref[...] = up_proj()

    @pl.when(f == nf)
    def _():
        _accumulate_dot(o_ref, act_ref[...], wo_ref)
        _postnorm_residual_inplace(o_ref, x_ref, gpost_ref)


def ffn(x, g_pre, w_in, w_out, g_post, *, layer=0, tm_pref=512, tf_pref=256):
    s, d = x.shape
    dff = w_out.shape[-2]
    tm, tf = _tile(s, tm_pref, SUBLANES), _tile(dff, tf_pref)
    nf = dff // tf
    up_tile = lambda f: jnp.minimum(f, nf - 1)
    down_tile = lambda f: jnp.maximum(f - 1, 0)
    return pl.pallas_call(
        _ffn_kernel,
        grid=(s // tm, nf + 1),
        in_specs=[
            pl.BlockSpec((tm, d), lambda i, f: (i, 0)),
            pl.BlockSpec((1, d), lambda i, f: (0, 0)),
            _wspec(w_in, layer, (d, tf), lambda i, f: (0, up_tile(f))),
            _wspec(w_in, layer, (d, tf), lambda i, f: (0, up_tile(f) + nf)),
            _wspec(w_out, layer, (tf, d), lambda i, f: (down_tile(f), 0)),
            pl.BlockSpec((1, d), lambda i, f: (0, 0)),
        ],
        out_specs=pl.BlockSpec((tm, d), lambda i, f: (i, 0)),
        out_shape=jax.ShapeDtypeStruct((s, d), F32),
        scratch_shapes=[pltpu.VMEM((tm, d), BF16), pltpu.VMEM((tm, tf), BF16)],
        compiler_params=_params(("parallel", "arbitrary")),
        name="ffn",
    )(x, g_pre.reshape(1, d), w_in, w_in, w_out, g_post.reshape(1, d))


CONV_HALO = 32
CONV_ROWS = 32


def _glu_conv_kernel(x_ref, g_ref, wv_ref, wg_ref, bv_ref, bg_ref, dw_ref, dwb_ref, y_ref,
                     xn_ref, win_ref, carry_ref, *, nj):
    s = pl.program_id(0)
    total = pl.num_programs(0) - 1
    tm = x_ref.shape[0]
    tn = wv_ref.shape[1]
    kw = dw_ref.shape[0]
    lane_tiles = [slice(c, c + LANES) for c in range(0, tn, LANES)]

    @pl.when(s == 0)
    def _():
        win_ref[...] = jnp.zeros_like(win_ref)
        carry_ref[...] = jnp.zeros_like(carry_ref)

    @pl.when(jnp.logical_and(s % nj == 0, s < total))
    def _():
        _prenorm_to(xn_ref, x_ref, g_ref)

    jc = jnp.maximum(s - 1, 0) % nj
    win_ref[:, 0:CONV_HALO, :] = carry_ref[jc]
    base = CONV_HALO - (kw - 1)
    rows = min(CONV_ROWS, tm)
    for r0 in range(0, tm, rows):
        for c, sl in enumerate(lane_tiles):
            acc = jnp.broadcast_to(dwb_ref[:, sl], (rows, LANES))
            for k in range(kw):
                acc = acc + dw_ref[k:k + 1, sl] * win_ref[c, base + r0 + k:base + r0 + k + rows, :]
            y_ref[r0:r0 + rows, sl] = acc
    carry_ref[jc] = win_ref[:, tm:tm + CONV_HALO, :]

    xn = xn_ref[...]
    val = jnp.dot(xn, wv_ref[...], preferred_element_type=F32) + bv_ref[...]
    gate = jnp.dot(xn, wg_ref[...], preferred_element_type=F32) + bg_ref[...]
    h = val * _sigmoid(gate)
    for c, sl in enumerate(lane_tiles):
        win_ref[c, CONV_HALO:CONV_HALO + tm, :] = h[:, sl]


def glu_dwconv(x, g, w, b, dw, dw_b, *, layer=0, tm_pref=512, tn_pref=512):
    s, d = x.shape
    n = w.shape[-1] // 2
    kw = dw.shape[0]
    tm, tn = _tile(s, tm_pref, CONV_HALO), _tile(n, tn_pref)
    assert kw - 1 <= CONV_HALO <= tm and tn % LANES == 0
    ni, nj = s // tm, n // tn
    total = ni * nj
    b2 = b.reshape(1, 2 * n)

    def proj(step):
        t = jnp.minimum(step, total - 1)
        return t // nj, t % nj

    def conv(step):
        t = jnp.maximum(step - 1, 0)
        return t // nj, t % nj

    return pl.pallas_call(
        functools.partial(_glu_conv_kernel, nj=nj),
        grid=(total + 1,),
        in_specs=[
            pl.BlockSpec((tm, d), lambda q: (proj(q)[0], 0)),
            pl.BlockSpec((1, d), lambda q: (0, 0)),
            _wspec(w, layer, (d, tn), lambda q: (0, proj(q)[1])),
            _wspec(w, layer, (d, tn), lambda q: (0, proj(q)[1] + nj)),
            pl.BlockSpec((1, tn), lambda q: (0, proj(q)[1])),
            pl.BlockSpec((1, tn), lambda q: (0, proj(q)[1] + nj)),
            pl.BlockSpec((kw, tn), lambda q: (0, conv(q)[1])),
            pl.BlockSpec((1, tn), lambda q: (0, conv(q)[1])),
        ],
        out_specs=pl.BlockSpec((tm, tn), lambda q: conv(q)),
        out_shape=jax.ShapeDtypeStruct((s, n), F32),
        scratch_shapes=[
            pltpu.VMEM((tm, d), BF16),
            pltpu.VMEM((tn // LANES, CONV_HALO + tm, LANES), F32),
            pltpu.VMEM((nj, tn // LANES, CONV_HALO, LANES), F32),
        ],
        compiler_params=_params(("arbitrary",)),
        name="glu_dwconv",
    )(x, g.reshape(1, d), w, w, b2, b2, dw, dw_b.reshape(1, n))


def _ln_silu_kernel(y_ref, lng_ref, lnb_ref, o_ref):
    y = y_ref[...]
    mu = jnp.mean(y, axis=-1, keepdims=True)
    yc = y - mu
    var = jnp.mean(yc * yc, axis=-1, keepdims=True)
    z = yc * lax.rsqrt(var + EPS) * lng_ref[...] + lnb_ref[...]
    o_ref[...] = (z * _sigmoid(z)).astype(o_ref.dtype)


def ln_silu(y, ln_g, ln_b, *, t_pref=256):
    s, d = y.shape
    t = _tile(s, t_pref, SUBLANES)
    return pl.pallas_call(
        _ln_silu_kernel,
        grid=(s // t,),
        in_specs=[
            pl.BlockSpec((t, d), lambda i: (i, 0)),
            pl.BlockSpec((1, d), lambda i: (0, 0)),
            pl.BlockSpec((1, d), lambda i: (0, 0)),
        ],
        out_specs=pl.BlockSpec((t, d), lambda i: (i, 0)),
        out_shape=jax.ShapeDtypeStruct((s, d), BF16),
        compiler_params=_params(("parallel",)),
        name="ln_silu",
    )(y, ln_g.reshape(1, d), ln_b.reshape(1, d))


LRU_HALO = SUBLANES


def _lru_kernel(recprev_ref, rec_ref, gb_ref, cw_ref, cb_ref, wa_ref, wx_ref, ba_ref, bx_ref,
                lam_ref, y_ref, win_ref, a_ref, b_ref, h_ref, carry_ref):
    i = pl.program_id(1)
    t, c = rec_ref.shape
    kw = cw_ref.shape[0]

    @pl.when(i == 0)
    def _():
        carry_ref[...] = jnp.zeros_like(carry_ref)

    keep = (i > 0).astype(F32)
    win_ref[0:LRU_HALO, :] = recprev_ref[...] * keep
    win_ref[LRU_HALO:LRU_HALO + t, :] = rec_ref[...]
    base = LRU_HALO - (kw - 1)
    u = jnp.broadcast_to(cb_ref[...], (t, c))
    for k in range(kw):
        u = u + cw_ref[k:k + 1, :] * win_ref[base + k:base + k + t, :]

    ub = u.astype(BF16)
    r = _sigmoid(jnp.dot(ub, wa_ref[0], preferred_element_type=F32) + ba_ref[...])
    ig = _sigmoid(jnp.dot(ub, wx_ref[0], preferred_element_type=F32) + bx_ref[...])
    log_a = (-LRU_C) * r * _softplus(-lam_ref[...])
    a_ref[...] = jnp.exp(log_a)
    b_ref[...] = jnp.sqrt(1.0 - jnp.exp(2.0 * log_a)) * (ig * u)

    def step(tt, h):
        h = a_ref[pl.ds(tt, 1), :] * h + b_ref[pl.ds(tt, 1), :]
        h_ref[pl.ds(tt, 1), :] = h
        return h

    carry_ref[...] = lax.fori_loop(0, t, step, carry_ref[...], unroll=8)
    y_ref[...] = (h_ref[...] * _gelu_tanh(gb_ref[...])).astype(y_ref.dtype)


def lru_core(hin, conv_w, conv_b, wa, wx, ba, bx, lam, *, t_pref=256):
    s = hin.shape[0]
    r = hin.shape[1] // 2
    ng, c, _ = wa.shape
    kw = conv_w.shape[0]
    assert kw - 1 <= LRU_HALO and ng * c == r
    t = _tile(s, t_pref, LRU_HALO)
    ratio = t // LRU_HALO
    row = lambda v: v.reshape(1, r)
    vec_spec = pl.BlockSpec((1, c), lambda g, i: (0, g))
    return pl.pallas_call(
        _lru_kernel,
        grid=(ng, s // t),
        in_specs=[
            pl.BlockSpec((LRU_HALO, c), lambda g, i: (jnp.maximum(i * ratio - 1, 0), ng + g)),
            pl.BlockSpec((t, c), lambda g, i: (i, ng + g)),
            pl.BlockSpec((t, c), lambda g, i: (i, g)),
            pl.BlockSpec((kw, c), lambda g, i: (0, g)),
            vec_spec,
            pl.BlockSpec((1, c, c), lambda g, i: (g, 0, 0)),
            pl.BlockSpec((1, c, c), lambda g, i: (g, 0, 0)),
            vec_spec, vec_spec, vec_spec,
        ],
        out_specs=pl.BlockSpec((t, c), lambda g, i: (i, g)),
        out_shape=jax.ShapeDtypeStruct((s, r), BF16),
        scratch_shapes=[
            pltpu.VMEM((LRU_HALO + t, c), F32),
            pltpu.VMEM((t, c), F32),
            pltpu.VMEM((t, c), F32),
            pltpu.VMEM((t, c), F32),
            pltpu.VMEM((1, c), F32),
        ],
        compiler_params=_params(("parallel", "arbitrary")),
        name="lru_core",
    )(hin, hin, hin, conv_w, row(conv_b), wa, wx, row(ba), row(bx), row(lam))


def _block_diag_groups(w, heads_per_group):
    h, hd, _ = w.shape
    g = heads_per_group
    wg = w.reshape(h // g, g, hd, hd)
    eye = jnp.eye(g, dtype=w.dtype)
    return jnp.einsum('ngij,gh->ngihj', wg, eye).reshape(h // g, g * hd, g * hd)


def _heads_per_group(n_heads, head_dim):
    g = 1
    while (g * head_dim) % LANES != 0 and g < n_heads:
        g += 1
    assert n_heads % g == 0
    return g


def _mlstm_kernel(q_ref, k_ref, v_ref, o_ref, gates_ref, gbias_ref, hg_ref, out_ref,
                  c_ref, n_ref, m_ref, *, n_heads):
    ci = pl.program_id(0)
    L = q_ref.shape[0]
    dqk = q_ref.shape[1] // n_heads
    dv = v_ref.shape[1] // n_heads
    scale = dqk ** -0.5

    @pl.when(ci == 0)
    def _():
        c_ref[...] = jnp.zeros_like(c_ref)
        n_ref[...] = jnp.zeros_like(n_ref)
        m_ref[...] = jnp.zeros_like(m_ref)

    gz = gates_ref[...] + gbias_ref[...]
    lane = lax.broadcasted_iota(jnp.int32, gz.shape, 1)
    glog = jnp.where(lane < n_heads, gz, -_softplus(-gz))
    glog_t = glog.T

    rowi = lax.broadcasted_iota(jnp.int32, (L, L), 0)
    coli = lax.broadcasted_iota(jnp.int32, (L, L), 1)
    causal = coli <= rowi

    for h in range(n_heads):
        li_col = glog[:, h:h + 1]
        li_row = glog_t[h:h + 1, :]
        lf_col = glog[:, n_heads + h:n_heads + h + 1]
        lf_row = glog_t[n_heads + h:n_heads + h + 1, :]
        b_col = jnp.sum(jnp.where(causal, lf_row, 0.0), axis=1, keepdims=True)
        b_row = jnp.sum(jnp.where(rowi <= coli, lf_col, 0.0), axis=0, keepdims=True)
        m_st = m_ref[h][0:1, 0:1]
        d_mat = jnp.where(causal, b_col - b_row + li_row, -jnp.inf)
        m_inter = b_col + m_st
        m_out = jnp.maximum(m_inter, jnp.max(d_mat, axis=1, keepdims=True))
        w_inter = jnp.exp(m_inter - m_out)

        qh = q_ref[:, h * dqk:(h + 1) * dqk]
        kh = k_ref[:, h * dqk:(h + 1) * dqk] * scale
        vh = v_ref[:, h * dv:(h + 1) * dv]
        qb, kb, vb = qh.astype(BF16), kh.astype(BF16), vh.astype(BF16)

        qk = lax.dot_general(qb, kb, (((1,), (1,)), ((), ())), preferred_element_type=F32)
        s_mat = qk * jnp.exp(d_mat - m_out)
        c_st = c_ref[h]
        n_st = n_ref[h]
        num = w_inter * jnp.dot(qb, c_st.astype(BF16), preferred_element_type=F32) \
            + jnp.dot(s_mat.astype(BF16), vb, preferred_element_type=F32)
        den = w_inter * jnp.sum(qh * n_st, axis=1, keepdims=True) + jnp.sum(s_mat, axis=1, keepdims=True)
        hh = num / jnp.maximum(jnp.abs(den), jnp.exp(-m_out))

        b_last = b_col[L - 1:L, :]
        a_s = b_last - b_col + li_col
        m_new = jnp.maximum(b_last + m_st, jnp.max(a_s, axis=0, keepdims=True))
        w_old = jnp.exp(b_last + m_st - m_new)
        w_s = jnp.exp(a_s - m_new)
        kw_ = kh * w_s
        c_ref[h] = w_old * c_st + lax.dot_general(kw_.astype(BF16), vb, (((0,), (0,)), ((), ())),
                                                  preferred_element_type=F32)
        n_ref[h] = w_old * n_st + jnp.sum(kw_, axis=0, keepdims=True)
        m_ref[h] = jnp.broadcast_to(m_new, m_ref.shape[1:])

        hn = hh * lax.rsqrt(jnp.mean(hh * hh, axis=-1, keepdims=True) + EPS) * hg_ref[:, h * dv:(h + 1) * dv]
        out_ref[:, h * dv:(h + 1) * dv] = (hn * _sigmoid(o_ref[:, h * dv:(h + 1) * dv])).astype(out_ref.dtype)


def mlstm_core(p, gates, gate_bias_row, head_g, *, n_heads, qk, v):
    s = p.shape[0]
    L = _tile(s, MLSTM_CHUNK, SUBLANES)
    assert (2 * qk) % v == 0
    voff = (2 * qk) // v
    dqk, dv = qk // n_heads, v // n_heads
    return pl.pallas_call(
        functools.partial(_mlstm_kernel, n_heads=n_heads),
        grid=(s // L,),
        in_specs=[
            pl.BlockSpec((L, qk), lambda c: (c, 0)),
            pl.BlockSpec((L, qk), lambda c: (c, 1)),
            pl.BlockSpec((L, v), lambda c: (c, voff)),
            pl.BlockSpec((L, v), lambda c: (c, voff + 1)),
            pl.BlockSpec((L, LANES), lambda c: (c, 0)),
            pl.BlockSpec((1, LANES), lambda c: (0, 0)),
            pl.BlockSpec((1, v), lambda c: (0, 0)),
        ],
        out_specs=pl.BlockSpec((L, v), lambda c: (c, 0)),
        out_shape=jax.ShapeDtypeStruct((s, v), BF16),
        scratch_shapes=[
            pltpu.VMEM((n_heads, dqk, dv), F32),
            pltpu.VMEM((n_heads, 1, dqk), F32),
            pltpu.VMEM((n_heads, SUBLANES, LANES), F32),
        ],
        compiler_params=_params(("arbitrary",)),
        name="mlstm_core",
    )(p, p, p, p, gates, gate_bias_row, head_g.reshape(1, v))


def _conv_layer(x, g_pre, g_post, layer, w_in, b_in, dw, dw_b, ln_g, ln_b, w_out, b_out):
    y = glu_dwconv(x, g_pre, w_in, b_in, dw, dw_b, layer=layer)
    a = ln_silu(y, ln_g, ln_b)
    return matmul_postnorm_residual(a, w_out, b_out, x, g_post, layer=layer)


def _lru_layer(x, g_pre, g_post, layer, w_in, b_in, conv_w, conv_b, gate_a_w, gate_x_w, gate_a_b, gate_x_b,
               lam, w_out, b_out):
    n_heads, head_dim, _ = gate_a_w.shape
    hpg = _heads_per_group(n_heads, head_dim)
    wa = _block_diag_groups(gate_a_w, hpg).astype(BF16)
    wx = _block_diag_groups(gate_x_w, hpg).astype(BF16)
    hin = norm_matmul(x, g_pre, w_in, b_in, layer=layer)
    y = lru_core(hin, conv_w, conv_b, wa, wx, gate_a_b, gate_x_b, lam)
    return matmul_postnorm_residual(y, w_out, b_out, x, g_post, layer=layer)


def _mlstm_layer(x, g_pre, g_post, w_in, gate_b, head_g, w_out):
    d = x.shape[1]
    n_heads = gate_b.shape[1]
    v = w_out.shape[0]
    proj = w_in.shape[1]
    qk = (proj - 2 * v - 2 * n_heads) // 2
    main = 2 * qk + 2 * v
    assert 2 * n_heads <= LANES
    w_main = w_in[:, :main].astype(BF16)
    w_gate = jnp.pad(w_in[:, main:], ((0, 0), (0, LANES - 2 * n_heads))).astype(BF16)
    p = norm_matmul(x, g_pre, w_main, jnp.zeros((main,), F32))
    gates = norm_matmul(x, g_pre, w_gate, jnp.zeros((LANES,), F32))
    gbias = jnp.pad(gate_b.reshape(1, 2 * n_heads), ((0, 0), (0, LANES - 2 * n_heads)))
    hcore = mlstm_core(p, gates, gbias, head_g, n_heads=n_heads, qk=qk, v=v)
    return matmul_postnorm_residual(hcore, w_out.astype(BF16), jnp.zeros((d,), F32), x, g_post)


def kernel(x, norm_g, ffn_w_in, ffn_w_out, conv_w_in, conv_b_in, conv_dw, conv_dw_b, conv_ln_g, conv_ln_b, conv_w_out, conv_b_out, lru_w_in, lru_b_in, lru_conv_w, lru_conv_b, lru_gate_a_w, lru_gate_x_w, lru_gate_a_b, lru_gate_x_b, lru_lambda, lru_w_out, lru_b_out, mlstm_w_in, mlstm_gate_b, mlstm_head_g, mlstm_w_out):
    bsz, seq, d = x.shape
    assert bsz == 1, "sequence mixers are written for a single sequence"
    xs = x.reshape(seq, d)
    ffn_in, ffn_out = ffn_w_in.astype(BF16), ffn_w_out.astype(BF16)
    conv_in, conv_out = conv_w_in.astype(BF16), conv_w_out.astype(BF16)
    lru_in, lru_out = lru_w_in.astype(BF16), lru_w_out.astype(BF16)
    ia = ib = ic = 0
    for i in range(norm_g.shape[0]):
        kind = i % N_MIXERS
        g = norm_g[i]
        if kind == 0:
            xs = _conv_layer(xs, g[0], g[1], ia, conv_in, conv_b_in[ia], conv_dw[ia], conv_dw_b[ia],
                             conv_ln_g[ia], conv_ln_b[ia], conv_out, conv_b_out[ia])
            ia += 1
        elif kind == 1:
            xs = _lru_layer(xs, g[0], g[1], ib, lru_in, lru_b_in[ib], lru_conv_w[ib], lru_conv_b[ib],
                            lru_gate_a_w[ib], lru_gate_x_w[ib], lru_gate_a_b[ib], lru_gate_x_b[ib],
                            lru_lambda[ib], lru_out, lru_b_out[ib])
            ib += 1
        else:
            xs = _mlstm_layer(xs, g[0], g[1], mlstm_w_in[ic], mlstm_gate_b[ic], mlstm_head_g[ic],
                              mlstm_w_out[ic])
            ic += 1
        xs = ffn(xs, g[2], ffn_in, ffn_out, g[3], layer=i)
    return xs.reshape(bsz, seq, d)
```

```python
import functools
import math

import jax
import jax.numpy as jnp
from jax import lax
from jax.experimental import pallas as pl
from jax.experimental.pallas import tpu as pltpu

EPS = 1e-6
LRU_C = 8.0
N_MIXERS = 3
MLSTM_CHUNK = 256
LANES = 128
SUBLANES = 8
VMEM_LIMIT_BYTES = 56 * 1024 * 1024

F32 = jnp.float32
BF16 = jnp.bfloat16


def _tile(n, pref, align=LANES):
    if n <= pref:
        return n
    t = (pref // align) * align
    while t >= align:
        if n % t == 0:
            return t
        t -= align
    return n


def _wspec(w, layer, block, index_map):
    if w.ndim == 2:
        return pl.BlockSpec(block, index_map)
    return pl.BlockSpec((None,) + block, lambda *ids: (layer,) + index_map(*ids))


def _params(sem):
    return pltpu.CompilerParams(dimension_semantics=sem, vmem_limit_bytes=VMEM_LIMIT_BYTES)


def _sigmoid(z):
    return 1.0 / (1.0 + jnp.exp(-z))


def _softplus(z):
    return jnp.maximum(z, 0.0) + jnp.log1p(jnp.exp(-jnp.abs(z)))


def _gelu_tanh(z):
    c = math.sqrt(2.0 / math.pi)
    return 0.5 * z * (1.0 + jnp.tanh(c * (z + 0.044715 * (z * z * z))))


def _rms(y, g):
    ms = jnp.mean(y * y, axis=-1, keepdims=True)
    return y * lax.rsqrt(ms + EPS) * g


COL_CHUNK = 1024


def _col_chunks(d):
    step = min(COL_CHUNK, d)
    return [(c, min(step, d - c)) for c in range(0, d, step)]


def _inv_rms(load_chunk, d):
    ss = None
    for c, w in _col_chunks(d):
        y = load_chunk(c, w)
        part = jnp.sum(y * y, axis=-1, keepdims=True)
        ss = part if ss is None else ss + part
    return lax.rsqrt(ss * (1.0 / d) + EPS)


def _prenorm_to(xn_ref, x_ref, g_ref):
    d = x_ref.shape[1]
    inv = _inv_rms(lambda c, w: x_ref[:, c:c + w], d)
    for c, w in _col_chunks(d):
        xn_ref[:, c:c + w] = (x_ref[:, c:c + w] * inv * g_ref[:, c:c + w]).astype(xn_ref.dtype)


def _postnorm_residual_inplace(o_ref, xres_ref, g_ref, b_ref=None):
    d = o_ref.shape[1]

    def load(c, w):
        y = o_ref[:, c:c + w]
        return y if b_ref is None else y + b_ref[:, c:c + w]

    inv = _inv_rms(load, d)
    for c, w in _col_chunks(d):
        o_ref[:, c:c + w] = xres_ref[:, c:c + w] + load(c, w) * inv * g_ref[:, c:c + w]


def _accumulate_dot(o_ref, a, w_ref):
    d = o_ref.shape[1]
    for c, w in _col_chunks(d):
        o_ref[:, c:c + w] += jnp.dot(a, w_ref[:, c:c + w], preferred_element_type=F32)


def _norm_mm_kernel(x_ref, g_ref, w_ref, b_ref, o_ref, xn_ref):
    @pl.when(pl.program_id(1) == 0)
    def _():
        _prenorm_to(xn_ref, x_ref, g_ref)

    o_ref[...] = jnp.dot(xn_ref[...], w_ref[...], preferred_element_type=F32) + b_ref[...]


def norm_matmul(x, g, w, b, *, layer=0, tm_pref=512, tn_pref=1024):
    s, d = x.shape
    n = w.shape[-1]
    tm, tn = _tile(s, tm_pref, SUBLANES), _tile(n, tn_pref)
    return pl.pallas_call(
        _norm_mm_kernel,
        grid=(s // tm, n // tn),
        in_specs=[
            pl.BlockSpec((tm, d), lambda i, j: (i, 0)),
            pl.BlockSpec((1, d), lambda i, j: (0, 0)),
            _wspec(w, layer, (d, tn), lambda i, j: (0, j)),
            pl.BlockSpec((1, tn), lambda i, j: (0, j)),
        ],
        out_specs=pl.BlockSpec((tm, tn), lambda i, j: (i, j)),
        out_shape=jax.ShapeDtypeStruct((s, n), F32),
        scratch_shapes=[pltpu.VMEM((tm, d), BF16)],
        compiler_params=_params(("parallel", "arbitrary")),
        name="norm_matmul",
    )(x, g.reshape(1, d), w, b.reshape(1, n))


def _mm_postnorm_kernel(a_ref, w_ref, b_ref, xres_ref, g_ref, o_ref):
    k = pl.program_id(1)

    @pl.when(k == 0)
    def _():
        o_ref[...] = jnp.zeros_like(o_ref)

    _accumulate_dot(o_ref, a_ref[...].astype(BF16), w_ref)

    @pl.when(k == pl.num_programs(1) - 1)
    def _():
        _postnorm_residual_inplace(o_ref, xres_ref, g_ref, b_ref)


def matmul_postnorm_residual(a, w, b, xres, g, *, layer=0, tm_pref=512, tk_pref=1024):
    s, kdim = a.shape
    d = w.shape[-1]
    tm, tk = _tile(s, tm_pref, SUBLANES), _tile(kdim, tk_pref)
    return pl.pallas_call(
        _mm_postnorm_kernel,
        grid=(s // tm, kdim // tk),
        in_specs=[
            pl.BlockSpec((tm, tk), lambda i, k: (i, k)),
            _wspec(w, layer, (tk, d), lambda i, k: (k, 0)),
            pl.BlockSpec((1, d), lambda i, k: (0, 0)),
            pl.BlockSpec((tm, d), lambda i, k: (i, 0)),
            pl.BlockSpec((1, d), lambda i, k: (0, 0)),
        ],
        out_specs=pl.BlockSpec((tm, d), lambda i, k: (i, 0)),
        out_shape=jax.ShapeDtypeStruct((s, d), F32),
        compiler_params=_params(("parallel", "arbitrary")),
        name="matmul_postnorm_residual",
    )(a, w, b.reshape(1, d), xres, g.reshape(1, d))


def _ffn_kernel(x_ref, gpre_ref, wg_ref, wu_ref, wo_ref, gpost_ref, o_ref, xn_ref):
    f = pl.program_id(1)

    @pl.when(f == 0)
    def _():
        _prenorm_to(xn_ref, x_ref, gpre_ref)
        o_ref[...] = jnp.zeros_like(o_ref)

    xn = xn_ref[...]
    gate = jnp.dot(xn, wg_ref[...], preferred_element_type=F32)
    up = jnp.dot(xn, wu_ref[...], preferred_element_type=F32)
    act = (gate * _sigmoid(gate) * up).astype(BF16)
    _accumulate_dot(o_ref, act, wo_ref)

    @pl.when(f == pl.num_programs(1) - 1)
    def _():
        _postnorm_residual_inplace(o_ref, x_ref, gpost_ref)


def ffn(x, g_pre, w_in, w_out, g_post, *, layer=0, tm_pref=512, tf_pref=256):
    s, d = x.shape
    dff = w_out.shape[-2]
    tm, tf = _tile(s, tm_pref, SUBLANES), _tile(dff, tf_pref)
    nf = dff // tf
    return pl.pallas_call(
        _ffn_kernel,
        grid=(s // tm, nf),
        in_specs=[
            pl.BlockSpec((tm, d), lambda i, f: (i, 0)),
            pl.BlockSpec((1, d), lambda i, f: (0, 0)),
            _wspec(w_in, layer, (d, tf), lambda i, f: (0, f)),
            _wspec(w_in, layer, (d, tf), lambda i, f: (0, f + nf)),
            _wspec(w_out, layer, (tf, d), lambda i, f: (f, 0)),
            pl.BlockSpec((1, d), lambda i, f: (0, 0)),
        ],
        out_specs=pl.BlockSpec((tm, d), lambda i, f: (i, 0)),
        out_shape=jax.ShapeDtypeStruct((s, d), F32),
        scratch_shapes=[pltpu.VMEM((tm, d), BF16)],
        compiler_params=_params(("parallel", "arbitrary")),
        name="ffn",
    )(x, g_pre.reshape(1, d), w_in, w_in, w_out, g_post.reshape(1, d))


CONV_HALO = 32
CONV_ROWS = 32
GLU_CONV_STAGES = 8
MXU_K_TILE = 256


def _anchor_zero(v):
    u = pltpu.bitcast(v, jnp.uint32)
    z = lax.shift_right_logical(lax.shift_right_logical(u, jnp.uint32(16)), jnp.uint32(16))
    return pltpu.bitcast(z, F32)


def _glu_conv_kernel(x_ref, g_ref, wv_ref, wg_ref, bv_ref, bg_ref, dw_ref, dwb_ref, y_ref,
                     xn_ref, win_ref, carry_ref, *, nj):
    s = pl.program_id(0)
    total = pl.num_programs(0) - 1
    tm = x_ref.shape[0]
    tn = wv_ref.shape[1]
    kw = dw_ref.shape[0]
    lane_tiles = [slice(c, c + LANES) for c in range(0, tn, LANES)]

    @pl.when(s == 0)
    def _():
        win_ref[...] = jnp.zeros_like(win_ref)
        carry_ref[...] = jnp.zeros_like(carry_ref)

    @pl.when(jnp.logical_and(s % nj == 0, s < total))
    def _():
        _prenorm_to(xn_ref, x_ref, g_ref)

    jc = jnp.maximum(s - 1, 0) % nj
    win_ref[:, 0:CONV_HALO, :] = carry_ref[jc]
    base = CONV_HALO - (kw - 1)
    rows = min(CONV_ROWS, tm)
    row_starts = list(range(0, tm, rows))
    d = xn_ref.shape[1]
    stages = max(1, min(GLU_CONV_STAGES, len(row_starts), d // MXU_K_TILE))
    assert len(row_starts) % stages == 0 and d % stages == 0
    per_stage = len(row_starts) // stages
    kc = d // stages
    val = gate = None
    conv_done = dot_done = None
    for g in range(stages):
        xk = xn_ref[:, g * kc:(g + 1) * kc]
        if conv_done is not None:
            xk = xk + _anchor_zero(conv_done)[0:1, 0:1].astype(xk.dtype)
        pv = jnp.dot(xk, wv_ref[g * kc:(g + 1) * kc, :], preferred_element_type=F32)
        pg = jnp.dot(xk, wg_ref[g * kc:(g + 1) * kc, :], preferred_element_type=F32)
        val = pv if val is None else val + pv
        gate = pg if gate is None else gate + pg

        prev = dot_done
        for r0 in row_starts[g * per_stage:(g + 1) * per_stage]:
            for c, sl in enumerate(lane_tiles):
                acc = jnp.broadcast_to(dwb_ref[:, sl], (rows, LANES))
                if prev is not None:
                    acc = acc + _anchor_zero(prev)[0:1, :]
                for k in range(kw):
                    acc = acc + dw_ref[k:k + 1, sl] * win_ref[c, base + r0 + k:base + r0 + k + rows, :]
                y_ref[r0:r0 + rows, sl] = acc
                prev = acc[0:SUBLANES, :]
        conv_done = prev
        dot_done = pg[tm - SUBLANES:tm, tn - LANES:tn]
    carry_ref[jc] = win_ref[:, tm:tm + CONV_HALO, :]

    h = (val + bv_ref[...]) * _sigmoid(gate + bg_ref[...])
    for c, sl in enumerate(lane_tiles):
        win_ref[c, CONV_HALO:CONV_HALO + tm, :] = h[:, sl]


def glu_dwconv(x, g, w, b, dw, dw_b, *, layer=0, tm_pref=512, tn_pref=512):
    s, d = x.shape
    n = w.shape[-1] // 2
    kw = dw.shape[0]
    tm, tn = _tile(s, tm_pref, CONV_HALO), _tile(n, tn_pref)
    assert kw - 1 <= CONV_HALO <= tm and tn % LANES == 0
    ni, nj = s // tm, n // tn
    total = ni * nj
    b2 = b.reshape(1, 2 * n)

    def proj(step):
        t = jnp.minimum(step, total - 1)
        return t // nj, t % nj

    def conv(step):
        t = jnp.maximum(step - 1, 0)
        return t // nj, t % nj

    return pl.pallas_call(
        functools.partial(_glu_conv_kernel, nj=nj),
        grid=(total + 1,),
        in_specs=[
            pl.BlockSpec((tm, d), lambda q: (proj(q)[0], 0)),
            pl.BlockSpec((1, d), lambda q: (0, 0)),
            _wspec(w, layer, (d, tn), lambda q: (0, proj(q)[1])),
            _wspec(w, layer, (d, tn), lambda q: (0, proj(q)[1] + nj)),
            pl.BlockSpec((1, tn), lambda q: (0, proj(q)[1])),
            pl.BlockSpec((1, tn), lambda q: (0, proj(q)[1] + nj)),
            pl.BlockSpec((kw, tn), lambda q: (0, conv(q)[1])),
            pl.BlockSpec((1, tn), lambda q: (0, conv(q)[1])),
        ],
        out_specs=pl.BlockSpec((tm, tn), lambda q: conv(q)),
        out_shape=jax.ShapeDtypeStruct((s, n), F32),
        scratch_shapes=[
            pltpu.VMEM((tm, d), BF16),
            pltpu.VMEM((tn // LANES, CONV_HALO + tm, LANES), F32),
            pltpu.VMEM((nj, tn // LANES, CONV_HALO, LANES), F32),
        ],
        compiler_params=_params(("arbitrary",)),
        name="glu_dwconv",
    )(x, g.reshape(1, d), w, w, b2, b2, dw, dw_b.reshape(1, n))


def _ln_silu_kernel(y_ref, lng_ref, lnb_ref, o_ref):
    y = y_ref[...]
    mu = jnp.mean(y, axis=-1, keepdims=True)
    yc = y - mu
    var = jnp.mean(yc * yc, axis=-1, keepdims=True)
    z = yc * lax.rsqrt(var + EPS) * lng_ref[...] + lnb_ref[...]
    o_ref[...] = (z * _sigmoid(z)).astype(o_ref.dtype)


def ln_silu(y, ln_g, ln_b, *, t_pref=256):
    s, d = y.shape
    t = _tile(s, t_pref, SUBLANES)
    return pl.pallas_call(
        _ln_silu_kernel,
        grid=(s // t,),
        in_specs=[
            pl.BlockSpec((t, d), lambda i: (i, 0)),
            pl.BlockSpec((1, d), lambda i: (0, 0)),
            pl.BlockSpec((1, d), lambda i: (0, 0)),
        ],
        out_specs=pl.BlockSpec((t, d), lambda i: (i, 0)),
        out_shape=jax.ShapeDtypeStruct((s, d), BF16),
        compiler_params=_params(("parallel",)),
        name="ln_silu",
    )(y, ln_g.reshape(1, d), ln_b.reshape(1, d))


LRU_HALO = SUBLANES


def _lru_kernel(recprev_ref, rec_ref, gb_ref, cw_ref, cb_ref, wa_ref, wx_ref, ba_ref, bx_ref,
                lam_ref, y_ref, win_ref, a_ref, b_ref, h_ref, carry_ref):
    i = pl.program_id(1)
    t, c = rec_ref.shape
    kw = cw_ref.shape[0]

    @pl.when(i == 0)
    def _():
        carry_ref[...] = jnp.zeros_like(carry_ref)

    keep = (i > 0).astype(F32)
    win_ref[0:LRU_HALO, :] = recprev_ref[...] * keep
    win_ref[LRU_HALO:LRU_HALO + t, :] = rec_ref[...]
    base = LRU_HALO - (kw - 1)
    u = jnp.broadcast_to(cb_ref[...], (t, c))
    for k in range(kw):
        u = u + cw_ref[k:k + 1, :] * win_ref[base + k:base + k + t, :]

    ub = u.astype(BF16)
    r = _sigmoid(jnp.dot(ub, wa_ref[0], preferred_element_type=F32) + ba_ref[...])
    ig = _sigmoid(jnp.dot(ub, wx_ref[0], preferred_element_type=F32) + bx_ref[...])
    log_a = (-LRU_C) * r * _softplus(-lam_ref[...])
    a_ref[...] = jnp.exp(log_a)
    b_ref[...] = jnp.sqrt(1.0 - jnp.exp(2.0 * log_a)) * (ig * u)

    def step(tt, h):
        h = a_ref[pl.ds(tt, 1), :] * h + b_ref[pl.ds(tt, 1), :]
        h_ref[pl.ds(tt, 1), :] = h
        return h

    carry_ref[...] = lax.fori_loop(0, t, step, carry_ref[...], unroll=8)
    y_ref[...] = (h_ref[...] * _gelu_tanh(gb_ref[...])).astype(y_ref.dtype)


def lru_core(hin, conv_w, conv_b, wa, wx, ba, bx, lam, *, t_pref=256):
    s = hin.shape[0]
    r = hin.shape[1] // 2
    ng, c, _ = wa.shape
    kw = conv_w.shape[0]
    assert kw - 1 <= LRU_HALO and ng * c == r
    t = _tile(s, t_pref, LRU_HALO)
    ratio = t // LRU_HALO
    row = lambda v: v.reshape(1, r)
    vec_spec = pl.BlockSpec((1, c), lambda g, i: (0, g))
    return pl.pallas_call(
        _lru_kernel,
        grid=(ng, s // t),
        in_specs=[
            pl.BlockSpec((LRU_HALO, c), lambda g, i: (jnp.maximum(i * ratio - 1, 0), ng + g)),
            pl.BlockSpec((t, c), lambda g, i: (i, ng + g)),
            pl.BlockSpec((t, c), lambda g, i: (i, g)),
            pl.BlockSpec((kw, c), lambda g, i: (0, g)),
            vec_spec,
            pl.BlockSpec((1, c, c), lambda g, i: (g, 0, 0)),
            pl.BlockSpec((1, c, c), lambda g, i: (g, 0, 0)),
            vec_spec, vec_spec, vec_spec,
        ],
        out_specs=pl.BlockSpec((t, c), lambda g, i: (i, g)),
        out_shape=jax.ShapeDtypeStruct((s, r), BF16),
        scratch_shapes=[
            pltpu.VMEM((LRU_HALO + t, c), F32),
            pltpu.VMEM((t, c), F32),
            pltpu.VMEM((t, c), F32),
            pltpu.VMEM((t, c), F32),
            pltpu.VMEM((1, c), F32),
        ],
        compiler_params=_params(("parallel", "arbitrary")),
        name="lru_core",
    )(hin, hin, hin, conv_w, row(conv_b), wa, wx, row(ba), row(bx), row(lam))


def _block_diag_groups(w, heads_per_group):
    h, hd, _ = w.shape
    g = heads_per_group
    wg = w.reshape(h // g, g, hd, hd)
    eye = jnp.eye(g, dtype=w.dtype)
    return jnp.einsum('ngij,gh->ngihj', wg, eye).reshape(h // g, g * hd, g * hd)


def _heads_per_group(n_heads, head_dim):
    g = 1
    while (g * head_dim) % LANES != 0 and g < n_heads:
        g += 1
    assert n_heads % g == 0
    return g


def _mlstm_kernel(q_ref, k_ref, v_ref, o_ref, gates_ref, gbias_ref, hg_ref, out_ref,
                  c_ref, n_ref, m_ref, *, n_heads):
    ci = pl.program_id(0)
    L = q_ref.shape[0]
    dqk = q_ref.shape[1] // n_heads
    dv = v_ref.shape[1] // n_heads
    scale = dqk ** -0.5

    @pl.when(ci == 0)
    def _():
        c_ref[...] = jnp.zeros_like(c_ref)
        n_ref[...] = jnp.zeros_like(n_ref)
        m_ref[...] = jnp.zeros_like(m_ref)

    gz = gates_ref[...] + gbias_ref[...]
    lane = lax.broadcasted_iota(jnp.int32, gz.shape, 1)
    glog = jnp.where(lane < n_heads, gz, -_softplus(-gz))
    glog_t = glog.T

    rowi = lax.broadcasted_iota(jnp.int32, (L, L), 0)
    coli = lax.broadcasted_iota(jnp.int32, (L, L), 1)
    causal = coli <= rowi

    for h in range(n_heads):
        li_col = glog[:, h:h + 1]
        li_row = glog_t[h:h + 1, :]
        lf_col = glog[:, n_heads + h:n_heads + h + 1]
        lf_row = glog_t[n_heads + h:n_heads + h + 1, :]
        b_col = jnp.sum(jnp.where(causal, lf_row, 0.0), axis=1, keepdims=True)
        b_row = jnp.sum(jnp.where(rowi <= coli, lf_col, 0.0), axis=0, keepdims=True)
        m_st = m_ref[h][0:1, 0:1]
        d_mat = jnp.where(causal, b_col - b_row + li_row, -jnp.inf)
        m_inter = b_col + m_st
        m_out = jnp.maximum(m_inter, jnp.max(d_mat, axis=1, keepdims=True))
        w_inter = jnp.exp(m_inter - m_out)

        qh = q_ref[:, h * dqk:(h + 1) * dqk]
        kh = k_ref[:, h * dqk:(h + 1) * dqk] * scale
        vh = v_ref[:, h * dv:(h + 1) * dv]
        qb, kb, vb = qh.astype(BF16), kh.astype(BF16), vh.astype(BF16)

        qk = lax.dot_general(qb, kb, (((1,), (1,)), ((), ())), preferred_element_type=F32)
        s_mat = qk * jnp.exp(d_mat - m_out)
        c_st = c_ref[h]
        n_st = n_ref[h]
        num = w_inter * jnp.dot(qb, c_st.astype(BF16), preferred_element_type=F32) \
            + jnp.dot(s_mat.astype(BF16), vb, preferred_element_type=F32)
        den = w_inter * jnp.sum(qh * n_st, axis=1, keepdims=True) + jnp.sum(s_mat, axis=1, keepdims=True)
        hh = num / jnp.maximum(jnp.abs(den), jnp.exp(-m_out))

        b_last = b_col[L - 1:L, :]
        a_s = b_last - b_col + li_col
        m_new = jnp.maximum(b_last + m_st, jnp.max(a_s, axis=0, keepdims=True))
        w_old = jnp.exp(b_last + m_st - m_new)
        w_s = jnp.exp(a_s - m_new)
        kw_ = kh * w_s
        c_ref[h] = w_old * c_st + lax.dot_general(kw_.astype(BF16), vb, (((0,), (0,)), ((), ())),
                                                  preferred_element_type=F32)
        n_ref[h] = w_old * n_st + jnp.sum(kw_, axis=0, keepdims=True)
        m_ref[h] = jnp.broadcast_to(m_new, m_ref.shape[1:])

        hn = hh * lax.rsqrt(jnp.mean(hh * hh, axis=-1, keepdims=True) + EPS) * hg_ref[:, h * dv:(h + 1) * dv]
        out_ref[:, h * dv:(h + 1) * dv] = (hn * _sigmoid(o_ref[:, h * dv:(h + 1) * dv])).astype(out_ref.dtype)


def mlstm_core(p, gates, gate_bias_row, head_g, *, n_heads, qk, v):
    s = p.shape[0]
    L = _tile(s, MLSTM_CHUNK, SUBLANES)
    assert (2 * qk) % v == 0
    voff = (2 * qk) // v
    dqk, dv = qk // n_heads, v // n_heads
    return pl.pallas_call(
        functools.partial(_mlstm_kernel, n_heads=n_heads),
        grid=(s // L,),
        in_specs=[
            pl.BlockSpec((L, qk), lambda c: (c, 0)),
            pl.BlockSpec((L, qk), lambda c: (c, 1)),
            pl.BlockSpec((L, v), lambda c: (c, voff)),
            pl.BlockSpec((L, v), lambda c: (c, voff + 1)),
            pl.BlockSpec((L, LANES), lambda c: (c, 0)),
            pl.BlockSpec((1, LANES), lambda c: (0, 0)),
            pl.BlockSpec((1, v), lambda c: (0, 0)),
        ],
        out_specs=pl.BlockSpec((L, v), lambda c: (c, 0)),
        out_shape=jax.ShapeDtypeStruct((s, v), BF16),
        scratch_shapes=[
            pltpu.VMEM((n_heads, dqk, dv), F32),
            pltpu.VMEM((n_heads, 1, dqk), F32),
            pltpu.VMEM((n_heads, SUBLANES, LANES), F32),
        ],
        compiler_params=_params(("arbitrary",)),
        name="mlstm_core",
    )(p, p, p, p, gates, gate_bias_row, head_g.reshape(1, v))


def _conv_layer(x, g_pre, g_post, layer, w_in, b_in, dw, dw_b, ln_g, ln_b, w_out, b_out):
    y = glu_dwconv(x, g_pre, w_in, b_in, dw, dw_b, layer=layer)
    a = ln_silu(y, ln_g, ln_b)
    return matmul_postnorm_residual(a, w_out, b_out, x, g_post, layer=layer)


def _lru_layer(x, g_pre, g_post, layer, w_in, b_in, conv_w, conv_b, gate_a_w, gate_x_w, gate_a_b, gate_x_b,
               lam, w_out, b_out):
    n_heads, head_dim, _ = gate_a_w.shape
    hpg = _heads_per_group(n_heads, head_dim)
    wa = _block_diag_groups(gate_a_w, hpg).astype(BF16)
    wx = _block_diag_groups(gate_x_w, hpg).astype(BF16)
    hin = norm_matmul(x, g_pre, w_in, b_in, layer=layer)
    y = lru_core(hin, conv_w, conv_b, wa, wx, gate_a_b, gate_x_b, lam)
    return matmul_postnorm_residual(y, w_out, b_out, x, g_post, layer=layer)


def _mlstm_layer(x, g_pre, g_post, w_in, gate_b, head_g, w_out):
    d = x.shape[1]
    n_heads = gate_b.shape[1]
    v = w_out.shape[0]
    proj = w_in.shape[1]
    qk = (proj - 2 * v - 2 * n_heads) // 2
    main = 2 * qk + 2 * v
    assert 2 * n_heads <= LANES
    w_main = w_in[:, :main].astype(BF16)
    w_gate = jnp.pad(w_in[:, main:], ((0, 0), (0, LANES - 2 * n_heads))).astype(BF16)
    p = norm_matmul(x, g_pre, w_main, jnp.zeros((main,), F32))
    gates = norm_matmul(x, g_pre, w_gate, jnp.zeros((LANES,), F32))
    gbias = jnp.pad(gate_b.reshape(1, 2 * n_heads), ((0, 0), (0, LANES - 2 * n_heads)))
    hcore = mlstm_core(p, gates, gbias, head_g, n_heads=n_heads, qk=qk, v=v)
    return matmul_postnorm_residual(hcore, w_out.astype(BF16), jnp.zeros((d,), F32), x, g_post)


def kernel(x, norm_g, ffn_w_in, ffn_w_out, conv_w_in, conv_b_in, conv_dw, conv_dw_b, conv_ln_g, conv_ln_b, conv_w_out, conv_b_out, lru_w_in, lru_b_in, lru_conv_w, lru_conv_b, lru_gate_a_w, lru_gate_x_w, lru_gate_a_b, lru_gate_x_b, lru_lambda, lru_w_out, lru_b_out, mlstm_w_in, mlstm_gate_b, mlstm_head_g, mlstm_w_out):
    bsz, seq, d = x.shape
    assert bsz == 1, "sequence mixers are written for a single sequence"
    xs = x.reshape(seq, d)
    ffn_in, ffn_out = ffn_w_in.astype(BF16), ffn_w_out.astype(BF16)
    conv_in, conv_out = conv_w_in.astype(BF16), conv_w_out.astype(BF16)
    lru_in, lru_out = lru_w_in.astype(BF16), lru_w_out.astype(BF16)
    ia = ib = ic = 0
    for i in range(norm_g.shape[0]):
        kind = i % N_MIXERS
        g = norm_g[i]
        if kind == 0:
            xs = _conv_layer(xs, g[0], g[1], ia, conv_in, conv_b_in[ia], conv_dw[ia], conv_dw_b[ia],
                             conv_ln_g[ia], conv_ln_b[ia], conv_out, conv_b_out[ia])
            ia += 1
        elif kind == 1:
            xs = _lru_layer(xs, g[0], g[1], ib, lru_in, lru_b_in[ib], lru_conv_w[ib], lru_conv_b[ib],
                            lru_gate_a_w[ib], lru_gate_x_w[ib], lru_gate_a_b[ib], lru_gate_x_b[ib],
                            lru_lambda[ib], lru_out, lru_b_out[ib])
            ib += 1
        else:
            xs = _mlstm_layer(xs, g[0], g[1], mlstm_w_in[ic], mlstm_gate_b[ic], mlstm_head_g[ic],
                              mlstm_w_out[ic])
            ic += 1
        xs = ffn(xs, g[2], ffn_in, ffn_out, g[3], layer=i)
    return xs.reshape(bsz, seq, d)
```

```python
import functools
import math

import jax
import jax.numpy as jnp
from jax import lax
from jax.experimental import pallas as pl
from jax.experimental.pallas import tpu as pltpu

EPS = 1e-6
LRU_C = 8.0
N_MIXERS = 3
MLSTM_CHUNK = 256
LANES = 128
SUBLANES = 8
VMEM_LIMIT_BYTES = 56 * 1024 * 1024

F32 = jnp.float32
BF16 = jnp.bfloat16


def _tile(n, pref, align=LANES):
    if n <= pref:
        return n
    t = (pref // align) * align
    while t >= align:
        if n % t == 0:
            return t
        t -= align
    return n


def _wspec(w, layer, block, index_map, **kwargs):
    if w.ndim == 2:
        return pl.BlockSpec(block, index_map, **kwargs)
    return pl.BlockSpec((None,) + block, lambda *ids: (layer,) + index_map(*ids), **kwargs)


def _params(sem, vmem_limit_bytes=VMEM_LIMIT_BYTES):
    return pltpu.CompilerParams(dimension_semantics=sem, vmem_limit_bytes=vmem_limit_bytes)


def _sigmoid(z):
    return 1.0 / (1.0 + jnp.exp(-z))


def _softplus(z):
    return jnp.maximum(z, 0.0) + jnp.log1p(jnp.exp(-jnp.abs(z)))


def _gelu_tanh(z):
    c = math.sqrt(2.0 / math.pi)
    return 0.5 * z * (1.0 + jnp.tanh(c * (z + 0.044715 * (z * z * z))))


def _rms(y, g):
    ms = jnp.mean(y * y, axis=-1, keepdims=True)
    return y * lax.rsqrt(ms + EPS) * g


COL_CHUNK = 1024


def _col_chunks(d):
    step = min(COL_CHUNK, d)
    return [(c, min(step, d - c)) for c in range(0, d, step)]


def _inv_rms(load_chunk, d):
    ss = None
    for c, w in _col_chunks(d):
        y = load_chunk(c, w)
        part = jnp.sum(y * y, axis=-1, keepdims=True)
        ss = part if ss is None else ss + part
    return lax.rsqrt(ss * (1.0 / d) + EPS)


def _prenorm_to(xn_ref, x_ref, g_ref):
    d = x_ref.shape[1]
    inv = _inv_rms(lambda c, w: x_ref[:, c:c + w], d)
    for c, w in _col_chunks(d):
        xn_ref[:, c:c + w] = (x_ref[:, c:c + w] * inv * g_ref[:, c:c + w]).astype(xn_ref.dtype)


def _postnorm_residual_inplace(o_ref, xres_ref, g_ref):
    d = o_ref.shape[1]
    inv = _inv_rms(lambda c, w: o_ref[:, c:c + w], d)
    for c, w in _col_chunks(d):
        o_ref[:, c:c + w] = xres_ref[:, c:c + w] + o_ref[:, c:c + w] * inv * g_ref[:, c:c + w]


def _accumulate_dot(o_ref, a, w_ref):
    d = o_ref.shape[1]
    for c, w in _col_chunks(d):
        o_ref[:, c:c + w] += jnp.dot(a, w_ref[:, c:c + w], preferred_element_type=F32)


def _norm_mm_kernel(x_ref, g_ref, w_ref, b_ref, o_ref, xn_ref):
    @pl.when(pl.program_id(1) == 0)
    def _():
        _prenorm_to(xn_ref, x_ref, g_ref)

    o_ref[...] = jnp.dot(xn_ref[...], w_ref[...], preferred_element_type=F32) + b_ref[...]


def norm_matmul(x, g, w, b, *, layer=0, tm_pref=512, tn_pref=1024):
    s, d = x.shape
    n = w.shape[-1]
    tm, tn = _tile(s, tm_pref, SUBLANES), _tile(n, tn_pref)
    return pl.pallas_call(
        _norm_mm_kernel,
        grid=(s // tm, n // tn),
        in_specs=[
            pl.BlockSpec((tm, d), lambda i, j: (i, 0)),
            pl.BlockSpec((1, d), lambda i, j: (0, 0)),
            _wspec(w, layer, (d, tn), lambda i, j: (0, j)),
            pl.BlockSpec((1, tn), lambda i, j: (0, j)),
        ],
        out_specs=pl.BlockSpec((tm, tn), lambda i, j: (i, j)),
        out_shape=jax.ShapeDtypeStruct((s, n), F32),
        scratch_shapes=[pltpu.VMEM((tm, d), BF16)],
        compiler_params=_params(("parallel", "arbitrary")),
        name="norm_matmul",
    )(x, g.reshape(1, d), w, b.reshape(1, n))


def _mm_postnorm_kernel(a_ref, w_ref, b_ref, xres_ref, g_ref, o_ref):
    k = pl.program_id(1)

    @pl.when(k == 0)
    def _():
        o_ref[...] = jnp.broadcast_to(b_ref[...], o_ref.shape)

    _accumulate_dot(o_ref, a_ref[...].astype(BF16), w_ref)

    @pl.when(k == pl.num_programs(1) - 1)
    def _():
        _postnorm_residual_inplace(o_ref, xres_ref, g_ref)


def _mm_postnorm_resident_kernel(a_ref, w_ref, b_ref, xres_ref, g_ref, o_ref):
    a = a_ref[...].astype(BF16)
    d = o_ref.shape[1]
    for c, w in _col_chunks(d):
        o_ref[:, c:c + w] = jnp.dot(a, w_ref[:, c:c + w], preferred_element_type=F32) + b_ref[:, c:c + w]
    _postnorm_residual_inplace(o_ref, xres_ref, g_ref)


RESIDENT_WEIGHT_MAX_BYTES = 32 * 1024 * 1024
RESIDENT_VMEM_LIMIT_BYTES = 60 * 1024 * 1024


def matmul_postnorm_residual(a, w, b, xres, g, *, layer=0, tm_pref=512, tk_pref=1024):
    s, kdim = a.shape
    d = w.shape[-1]
    if kdim * d * w.dtype.itemsize <= RESIDENT_WEIGHT_MAX_BYTES:
        tm = _tile(s, tm_pref // 2, SUBLANES)
        return pl.pallas_call(
            _mm_postnorm_resident_kernel,
            grid=(s // tm,),
            in_specs=[
                pl.BlockSpec((tm, kdim), lambda i: (i, 0)),
                _wspec(w, layer, (kdim, d), lambda i: (0, 0), pipeline_mode=pl.Buffered(1)),
                pl.BlockSpec((1, d), lambda i: (0, 0)),
                pl.BlockSpec((tm, d), lambda i: (i, 0)),
                pl.BlockSpec((1, d), lambda i: (0, 0)),
            ],
            out_specs=pl.BlockSpec((tm, d), lambda i: (i, 0)),
            out_shape=jax.ShapeDtypeStruct((s, d), F32),
            compiler_params=_params(("parallel",), RESIDENT_VMEM_LIMIT_BYTES),
            name="matmul_postnorm_residual_resident",
        )(a, w, b.reshape(1, d), xres, g.reshape(1, d))
    tm, tk = _tile(s, tm_pref, SUBLANES), _tile(kdim, tk_pref)
    return pl.pallas_call(
        _mm_postnorm_kernel,
        grid=(s // tm, kdim // tk),
        in_specs=[
            pl.BlockSpec((tm, tk), lambda i, k: (i, k)),
            _wspec(w, layer, (tk, d), lambda i, k: (k, 0)),
            pl.BlockSpec((1, d), lambda i, k: (0, 0)),
            pl.BlockSpec((tm, d), lambda i, k: (i, 0)),
            pl.BlockSpec((1, d), lambda i, k: (0, 0)),
        ],
        out_specs=pl.BlockSpec((tm, d), lambda i, k: (i, 0)),
        out_shape=jax.ShapeDtypeStruct((s, d), F32),
        compiler_params=_params(("parallel", "arbitrary")),
        name="matmul_postnorm_residual",
    )(a, w, b.reshape(1, d), xres, g.reshape(1, d))


def _ffn_kernel(x_ref, gpre_ref, wg_ref, wu_ref, wo_ref, gpost_ref, o_ref, xn_ref):
    f = pl.program_id(1)

    @pl.when(f == 0)
    def _():
        _prenorm_to(xn_ref, x_ref, gpre_ref)
        o_ref[...] = jnp.zeros_like(o_ref)

    xn = xn_ref[...]
    gate = jnp.dot(xn, wg_ref[...], preferred_element_type=F32)
    up = jnp.dot(xn, wu_ref[...], preferred_element_type=F32)
    act = (gate * _sigmoid(gate) * up).astype(BF16)
    _accumulate_dot(o_ref, act, wo_ref)

    @pl.when(f == pl.num_programs(1) - 1)
    def _():
        _postnorm_residual_inplace(o_ref, x_ref, gpost_ref)


def ffn(x, g_pre, w_in, w_out, g_post, *, layer=0, tm_pref=512, tf_pref=256):
    s, d = x.shape
    dff = w_out.shape[-2]
    tm, tf = _tile(s, tm_pref, SUBLANES), _tile(dff, tf_pref)
    nf = dff // tf
    return pl.pallas_call(
        _ffn_kernel,
        grid=(s // tm, nf),
        in_specs=[
            pl.BlockSpec((tm, d), lambda i, f: (i, 0)),
            pl.BlockSpec((1, d), lambda i, f: (0, 0)),
            _wspec(w_in, layer, (d, tf), lambda i, f: (0, f)),
            _wspec(w_in, layer, (d, tf), lambda i, f: (0, f + nf)),
            _wspec(w_out, layer, (tf, d), lambda i, f: (f, 0)),
            pl.BlockSpec((1, d), lambda i, f: (0, 0)),
        ],
        out_specs=pl.BlockSpec((tm, d), lambda i, f: (i, 0)),
        out_shape=jax.ShapeDtypeStruct((s, d), F32),
        scratch_shapes=[pltpu.VMEM((tm, d), BF16)],
        compiler_params=_params(("parallel", "arbitrary")),
        name="ffn",
    )(x, g_pre.reshape(1, d), w_in, w_in, w_out, g_post.reshape(1, d))


CONV_HALO = 32
CONV_ROWS = 32
GLU_CONV_STAGES = 8
MXU_K_TILE = 256


def _anchor_zero(v):
    u = pltpu.bitcast(v, jnp.uint32)
    z = lax.shift_right_logical(lax.shift_right_logical(u, jnp.uint32(16)), jnp.uint32(16))
    return pltpu.bitcast(z, F32)


def _glu_conv_kernel(x_ref, g_ref, wv_ref, wg_ref, bv_ref, bg_ref, dw_ref, dwb_ref, y_ref,
                     xn_ref, win_ref, carry_ref, *, nj):
    s = pl.program_id(0)
    total = pl.num_programs(0) - 1
    tm = x_ref.shape[0]
    tn = wv_ref.shape[1]
    kw = dw_ref.shape[0]
    lane_tiles = [slice(c, c + LANES) for c in range(0, tn, LANES)]

    @pl.when(s == 0)
    def _():
        win_ref[...] = jnp.zeros_like(win_ref)
        carry_ref[...] = jnp.zeros_like(carry_ref)

    @pl.when(jnp.logical_and(s % nj == 0, s < total))
    def _():
        _prenorm_to(xn_ref, x_ref, g_ref)

    jc = jnp.maximum(s - 1, 0) % nj
    win_ref[:, 0:CONV_HALO, :] = carry_ref[jc]
    base = CONV_HALO - (kw - 1)
    rows = min(CONV_ROWS, tm)
    row_starts = list(range(0, tm, rows))
    d = xn_ref.shape[1]
    stages = max(1, min(GLU_CONV_STAGES, len(row_starts), d // MXU_K_TILE))
    assert len(row_starts) % stages == 0 and d % stages == 0
    per_stage = len(row_starts) // stages
    kc = d // stages
    val = gate = None
    conv_done = dot_done = None
    for g in range(stages):
        xk = xn_ref[:, g * kc:(g + 1) * kc]
        if conv_done is not None:
            xk = xk + _anchor_zero(conv_done)[0:1, 0:1].astype(xk.dtype)
        pv = jnp.dot(xk, wv_ref[g * kc:(g + 1) * kc, :], preferred_element_type=F32)
        pg = jnp.dot(xk, wg_ref[g * kc:(g + 1) * kc, :], preferred_element_type=F32)
        val = pv if val is None else val + pv
        gate = pg if gate is None else gate + pg

        prev = dot_done
        for r0 in row_starts[g * per_stage:(g + 1) * per_stage]:
            for c, sl in enumerate(lane_tiles):
                acc = jnp.broadcast_to(dwb_ref[:, sl], (rows, LANES))
                if prev is not None:
                    acc = acc + _anchor_zero(prev)[0:1, :]
                for k in range(kw):
                    acc = acc + dw_ref[k:k + 1, sl] * win_ref[c, base + r0 + k:base + r0 + k + rows, :]
                y_ref[r0:r0 + rows, sl] = acc
                prev = acc[0:SUBLANES, :]
        conv_done = prev
        dot_done = pg[tm - SUBLANES:tm, tn - LANES:tn]
    carry_ref[jc] = win_ref[:, tm:tm + CONV_HALO, :]

    h = (val + bv_ref[...]) * _sigmoid(gate + bg_ref[...])
    for c, sl in enumerate(lane_tiles):
        win_ref[c, CONV_HALO:CONV_HALO + tm, :] = h[:, sl]


def glu_dwconv(x, g, w, b, dw, dw_b, *, layer=0, tm_pref=512, tn_pref=512):
    s, d = x.shape
    n = w.shape[-1] // 2
    kw = dw.shape[0]
    tm, tn = _tile(s, tm_pref, CONV_HALO), _tile(n, tn_pref)
    assert kw - 1 <= CONV_HALO <= tm and tn % LANES == 0
    ni, nj = s // tm, n // tn
    total = ni * nj
    b2 = b.reshape(1, 2 * n)

    def proj(step):
        t = jnp.minimum(step, total - 1)
        return t // nj, t % nj

    def conv(step):
        t = jnp.maximum(step - 1, 0)
        return t // nj, t % nj

    return pl.pallas_call(
        functools.partial(_glu_conv_kernel, nj=nj),
        grid=(total + 1,),
        in_specs=[
            pl.BlockSpec((tm, d), lambda q: (proj(q)[0], 0)),
            pl.BlockSpec((1, d), lambda q: (0, 0)),
            _wspec(w, layer, (d, tn), lambda q: (0, proj(q)[1])),
            _wspec(w, layer, (d, tn), lambda q: (0, proj(q)[1] + nj)),
            pl.BlockSpec((1, tn), lambda q: (0, proj(q)[1])),
            pl.BlockSpec((1, tn), lambda q: (0, proj(q)[1] + nj)),
            pl.BlockSpec((kw, tn), lambda q: (0, conv(q)[1])),
            pl.BlockSpec((1, tn), lambda q: (0, conv(q)[1])),
        ],
        out_specs=pl.BlockSpec((tm, tn), lambda q: conv(q)),
        out_shape=jax.ShapeDtypeStruct((s, n), F32),
        scratch_shapes=[
            pltpu.VMEM((tm, d), BF16),
            pltpu.VMEM((tn // LANES, CONV_HALO + tm, LANES), F32),
            pltpu.VMEM((nj, tn // LANES, CONV_HALO, LANES), F32),
        ],
        compiler_params=_params(("arbitrary",)),
        name="glu_dwconv",
    )(x, g.reshape(1, d), w, w, b2, b2, dw, dw_b.reshape(1, n))


def _ln_silu_kernel(y_ref, lng_ref, lnb_ref, o_ref):
    y = y_ref[...]
    mu = jnp.mean(y, axis=-1, keepdims=True)
    yc = y - mu
    var = jnp.mean(yc * yc, axis=-1, keepdims=True)
    z = yc * lax.rsqrt(var + EPS) * lng_ref[...] + lnb_ref[...]
    o_ref[...] = (z * _sigmoid(z)).astype(o_ref.dtype)


def ln_silu(y, ln_g, ln_b, *, t_pref=256):
    s, d = y.shape
    t = _tile(s, t_pref, SUBLANES)
    return pl.pallas_call(
        _ln_silu_kernel,
        grid=(s // t,),
        in_specs=[
            pl.BlockSpec((t, d), lambda i: (i, 0)),
            pl.BlockSpec((1, d), lambda i: (0, 0)),
            pl.BlockSpec((1, d), lambda i: (0, 0)),
        ],
        out_specs=pl.BlockSpec((t, d), lambda i: (i, 0)),
        out_shape=jax.ShapeDtypeStruct((s, d), BF16),
        compiler_params=_params(("parallel",)),
        name="ln_silu",
    )(y, ln_g.reshape(1, d), ln_b.reshape(1, d))


LRU_HALO = SUBLANES


def _lru_kernel(recprev_ref, rec_ref, gb_ref, cw_ref, cb_ref, wa_ref, wx_ref, ba_ref, bx_ref,
                lam_ref, y_ref, win_ref, a_ref, b_ref, h_ref, carry_ref):
    i = pl.program_id(1)
    t, c = rec_ref.shape
    kw = cw_ref.shape[0]

    @pl.when(i == 0)
    def _():
        carry_ref[...] = jnp.zeros_like(carry_ref)

    keep = (i > 0).astype(F32)
    base = LRU_HALO - (kw - 1)
    pieces = []
    for ci, c0 in enumerate(range(0, c, LANES)):
        sl = slice(c0, c0 + LANES)
        win_ref[ci, 0:LRU_HALO, :] = recprev_ref[:, sl] * keep
        win_ref[ci, LRU_HALO:LRU_HALO + t, :] = rec_ref[:, sl]
        piece = jnp.broadcast_to(cb_ref[:, sl], (t, LANES))
        for k in range(kw):
            piece = piece + cw_ref[k:k + 1, sl] * win_ref[ci, base + k:base + k + t, :]
        pieces.append(piece)
    u = jnp.concatenate(pieces, axis=1)

    ub = u.astype(BF16)
    r = _sigmoid(jnp.dot(ub, wa_ref[0], preferred_element_type=F32) + ba_ref[...])
    ig = _sigmoid(jnp.dot(ub, wx_ref[0], preferred_element_type=F32) + bx_ref[...])
    log_a = (-LRU_C) * r * _softplus(-lam_ref[...])
    a_ref[...] = jnp.exp(log_a)
    b_ref[...] = jnp.sqrt(1.0 - jnp.exp(2.0 * log_a)) * (ig * u)

    def step(tt, h):
        h = a_ref[pl.ds(tt, 1), :] * h + b_ref[pl.ds(tt, 1), :]
        h_ref[pl.ds(tt, 1), :] = h
        return h

    carry_ref[...] = lax.fori_loop(0, t, step, carry_ref[...], unroll=8)
    y_ref[...] = (h_ref[...] * _gelu_tanh(gb_ref[...])).astype(y_ref.dtype)


def lru_core(hin, conv_w, conv_b, wa, wx, ba, bx, lam, *, t_pref=256):
    s = hin.shape[0]
    r = hin.shape[1] // 2
    ng, c, _ = wa.shape
    kw = conv_w.shape[0]
    assert kw - 1 <= LRU_HALO and ng * c == r
    t = _tile(s, t_pref, LRU_HALO)
    ratio = t // LRU_HALO
    row = lambda v: v.reshape(1, r)
    vec_spec = pl.BlockSpec((1, c), lambda g, i: (0, g))
    return pl.pallas_call(
        _lru_kernel,
        grid=(ng, s // t),
        in_specs=[
            pl.BlockSpec((LRU_HALO, c), lambda g, i: (jnp.maximum(i * ratio - 1, 0), ng + g)),
            pl.BlockSpec((t, c), lambda g, i: (i, ng + g)),
            pl.BlockSpec((t, c), lambda g, i: (i, g)),
            pl.BlockSpec((kw, c), lambda g, i: (0, g)),
            vec_spec,
            pl.BlockSpec((1, c, c), lambda g, i: (g, 0, 0)),
            pl.BlockSpec((1, c, c), lambda g, i: (g, 0, 0)),
            vec_spec, vec_spec, vec_spec,
        ],
        out_specs=pl.BlockSpec((t, c), lambda g, i: (i, g)),
        out_shape=jax.ShapeDtypeStruct((s, r), BF16),
        scratch_shapes=[
            pltpu.VMEM((c // LANES, LRU_HALO + t, LANES), F32),
            pltpu.VMEM((t, c), F32),
            pltpu.VMEM((t, c), F32),
            pltpu.VMEM((t, c), F32),
            pltpu.VMEM((1, c), F32),
        ],
        compiler_params=_params(("parallel", "arbitrary")),
        name="lru_core",
    )(hin, hin, hin, conv_w, row(conv_b), wa, wx, row(ba), row(bx), row(lam))


def _block_diag_groups(w, heads_per_group):
    h, hd, _ = w.shape
    g = heads_per_group
    wg = w.reshape(h // g, g, hd, hd)
    eye = jnp.eye(g, dtype=w.dtype)
    return jnp.einsum('ngij,gh->ngihj', wg, eye).reshape(h // g, g * hd, g * hd)


def _heads_per_group(n_heads, head_dim):
    g = 1
    while (g * head_dim) % LANES != 0 and g < n_heads:
        g += 1
    assert n_heads % g == 0
    return g


def _mlstm_kernel(q_ref, k_ref, v_ref, o_ref, gates_ref, gbias_ref, hg_ref, out_ref,
                  c_ref, n_ref, m_ref, *, n_heads):
    ci = pl.program_id(0)
    L = q_ref.shape[0]
    dqk = q_ref.shape[1] // n_heads
    dv = v_ref.shape[1] // n_heads
    scale = dqk ** -0.5

    @pl.when(ci == 0)
    def _():
        c_ref[...] = jnp.zeros_like(c_ref)
        n_ref[...] = jnp.zeros_like(n_ref)
        m_ref[...] = jnp.zeros_like(m_ref)

    gz = gates_ref[...] + gbias_ref[...]
    lane = lax.broadcasted_iota(jnp.int32, gz.shape, 1)
    glog = jnp.where(lane < n_heads, gz, -_softplus(-gz))
    glog_t = glog.T

    rowi = lax.broadcasted_iota(jnp.int32, (L, L), 0)
    coli = lax.broadcasted_iota(jnp.int32, (L, L), 1)
    causal = coli <= rowi

    for h in range(n_heads):
        li_col = glog[:, h:h + 1]
        li_row = glog_t[h:h + 1, :]
        lf_col = glog[:, n_heads + h:n_heads + h + 1]
        lf_row = glog_t[n_heads + h:n_heads + h + 1, :]
        b_col = jnp.sum(jnp.where(causal, lf_row, 0.0), axis=1, keepdims=True)
        b_row = jnp.sum(jnp.where(rowi <= coli, lf_col, 0.0), axis=0, keepdims=True)
        m_st = m_ref[h][0:1, 0:1]
        d_mat = jnp.where(causal, b_col - b_row + li_row, -jnp.inf)
        m_inter = b_col + m_st
        m_out = jnp.maximum(m_inter, jnp.max(d_mat, axis=1, keepdims=True))
        w_inter = jnp.exp(m_inter - m_out)

        qh = q_ref[:, h * dqk:(h + 1) * dqk]
        kh = k_ref[:, h * dqk:(h + 1) * dqk] * scale
        vh = v_ref[:, h * dv:(h + 1) * dv]
        qb, kb, vb = qh.astype(BF16), kh.astype(BF16), vh.astype(BF16)

        qk = lax.dot_general(qb, kb, (((1,), (1,)), ((), ())), preferred_element_type=F32)
        s_mat = qk * jnp.exp(d_mat - m_out)
        c_st = c_ref[h]
        n_st = n_ref[h]
        num = w_inter * jnp.dot(qb, c_st.astype(BF16), preferred_element_type=F32) \
            + jnp.dot(s_mat.astype(BF16), vb, preferred_element_type=F32)
        den = w_inter * jnp.sum(qh * n_st, axis=1, keepdims=True) + jnp.sum(s_mat, axis=1, keepdims=True)
        hh = num / jnp.maximum(jnp.abs(den), jnp.exp(-m_out))

        b_last = b_col[L - 1:L, :]
        a_s = b_last - b_col + li_col
        m_new = jnp.maximum(b_last + m_st, jnp.max(a_s, axis=0, keepdims=True))
        w_old = jnp.exp(b_last + m_st - m_new)
        w_s = jnp.exp(a_s - m_new)
        kw_ = kh * w_s
        c_ref[h] = w_old * c_st + lax.dot_general(kw_.astype(BF16), vb, (((0,), (0,)), ((), ())),
                                                  preferred_element_type=F32)
        n_ref[h] = w_old * n_st + jnp.sum(kw_, axis=0, keepdims=True)
        m_ref[h] = jnp.broadcast_to(m_new, m_ref.shape[1:])

        hn = hh * lax.rsqrt(jnp.mean(hh * hh, axis=-1, keepdims=True) + EPS) * hg_ref[:, h * dv:(h + 1) * dv]
        out_ref[:, h * dv:(h + 1) * dv] = (hn * _sigmoid(o_ref[:, h * dv:(h + 1) * dv])).astype(out_ref.dtype)


def mlstm_core(p, gates, gate_bias_row, head_g, *, n_heads, qk, v):
    s = p.shape[0]
    L = _tile(s, MLSTM_CHUNK, SUBLANES)
    assert (2 * qk) % v == 0
    voff = (2 * qk) // v
    dqk, dv = qk // n_heads, v // n_heads
    return pl.pallas_call(
        functools.partial(_mlstm_kernel, n_heads=n_heads),
        grid=(s // L,),
        in_specs=[
            pl.BlockSpec((L, qk), lambda c: (c, 0)),
            pl.BlockSpec((L, qk), lambda c: (c, 1)),
            pl.BlockSpec((L, v), lambda c: (c, voff)),
            pl.BlockSpec((L, v), lambda c: (c, voff + 1)),
            pl.BlockSpec((L, LANES), lambda c: (c, 0)),
            pl.BlockSpec((1, LANES), lambda c: (0, 0)),
            pl.BlockSpec((1, v), lambda c: (0, 0)),
        ],
        out_specs=pl.BlockSpec((L, v), lambda c: (c, 0)),
        out_shape=jax.ShapeDtypeStruct((s, v), BF16),
        scratch_shapes=[
            pltpu.VMEM((n_heads, dqk, dv), F32),
            pltpu.VMEM((n_heads, 1, dqk), F32),
            pltpu.VMEM((n_heads, SUBLANES, LANES), F32),
        ],
        compiler_params=_params(("arbitrary",)),
        name="mlstm_core",
    )(p, p, p, p, gates, gate_bias_row, head_g.reshape(1, v))


def _conv_layer(x, g_pre, g_post, layer, w_in, b_in, dw, dw_b, ln_g, ln_b, w_out, b_out):
    y = glu_dwconv(x, g_pre, w_in, b_in, dw, dw_b, layer=layer)
    a = ln_silu(y, ln_g, ln_b)
    return matmul_postnorm_residual(a, w_out, b_out, x, g_post, layer=layer)


def _lru_layer(x, g_pre, g_post, layer, w_in, b_in, conv_w, conv_b, gate_a_w, gate_x_w, gate_a_b, gate_x_b,
               lam, w_out, b_out):
    n_heads, head_dim, _ = gate_a_w.shape
    hpg = _heads_per_group(n_heads, head_dim)
    wa = _block_diag_groups(gate_a_w, hpg).astype(BF16)
    wx = _block_diag_groups(gate_x_w, hpg).astype(BF16)
    hin = norm_matmul(x, g_pre, w_in, b_in, layer=layer)
    y = lru_core(hin, conv_w, conv_b, wa, wx, gate_a_b, gate_x_b, lam)
    return matmul_postnorm_residual(y, w_out, b_out, x, g_post, layer=layer)


def _mlstm_layer(x, g_pre, g_post, w_in, gate_b, head_g, w_out):
    d = x.shape[1]
    n_heads = gate_b.shape[1]
    v = w_out.shape[0]
    proj = w_in.shape[1]
    qk = (proj - 2 * v - 2 * n_heads) // 2
    main = 2 * qk + 2 * v
    assert 2 * n_heads <= LANES
    w_main = w_in[:, :main].astype(BF16)
    w_gate = jnp.pad(w_in[:, main:], ((0, 0), (0, LANES - 2 * n_heads))).astype(BF16)
    p = norm_matmul(x, g_pre, w_main, jnp.zeros((main,), F32))
    gates = norm_matmul(x, g_pre, w_gate, jnp.zeros((LANES,), F32))
    gbias = jnp.pad(gate_b.reshape(1, 2 * n_heads), ((0, 0), (0, LANES - 2 * n_heads)))
    hcore = mlstm_core(p, gates, gbias, head_g, n_heads=n_heads, qk=qk, v=v)
    return matmul_postnorm_residual(hcore, w_out.astype(BF16), jnp.zeros((d,), F32), x, g_post)


def kernel(x, norm_g, ffn_w_in, ffn_w_out, conv_w_in, conv_b_in, conv_dw, conv_dw_b, conv_ln_g, conv_ln_b, conv_w_out, conv_b_out, lru_w_in, lru_b_in, lru_conv_w, lru_conv_b, lru_gate_a_w, lru_gate_x_w, lru_gate_a_b, lru_gate_x_b, lru_lambda, lru_w_out, lru_b_out, mlstm_w_in, mlstm_gate_b, mlstm_head_g, mlstm_w_out):
    bsz, seq, d = x.shape
    assert bsz == 1, "sequence mixers are written for a single sequence"
    xs = x.reshape(seq, d)
    ffn_in, ffn_out = ffn_w_in.astype(BF16), ffn_w_out.astype(BF16)
    conv_in, conv_out = conv_w_in.astype(BF16), conv_w_out.astype(BF16)
    lru_in, lru_out = lru_w_in.astype(BF16), lru_w_out.astype(BF16)
    ia = ib = ic = 0
    for i in range(norm_g.shape[0]):
        kind = i % N_MIXERS
        g = norm_g[i]
        if kind == 0:
            xs = _conv_layer(xs, g[0], g[1], ia, conv_in, conv_b_in[ia], conv_dw[ia], conv_dw_b[ia],
                             conv_ln_g[ia], conv_ln_b[ia], conv_out, conv_b_out[ia])
            ia += 1
        elif kind == 1:
            xs = _lru_layer(xs, g[0], g[1], ib, lru_in, lru_b_in[ib], lru_conv_w[ib], lru_conv_b[ib],
                            lru_gate_a_w[ib], lru_gate_x_w[ib], lru_gate_a_b[ib], lru_gate_x_b[ib],
                            lru_lambda[ib], lru_out, lru_b_out[ib])
            ib += 1
        else:
            xs = _mlstm_layer(xs, g[0], g[1], mlstm_w_in[ic], mlstm_gate_b[ic], mlstm_head_g[ic],
                              mlstm_w_out[ic])
            ic += 1
        xs = ffn(xs, g[2], ffn_in, ffn_out, g[3], layer=i)
    return xs.reshape(bsz, seq, d)
```

```python
import functools
import math

import jax
import jax.numpy as jnp
from jax import lax
from jax.experimental import pallas as pl
from jax.experimental.pallas import tpu as pltpu

EPS = 1e-6
LRU_C = 8.0
N_MIXERS = 3
MLSTM_CHUNK = 256
LANES = 128
SUBLANES = 8
VMEM_LIMIT_BYTES = 56 * 1024 * 1024

F32 = jnp.float32
BF16 = jnp.bfloat16


def _tile(n, pref, align=LANES):
    if n <= pref:
        return n
    t = (pref // align) * align
    while t >= align:
        if n % t == 0:
            return t
        t -= align
    return n


def _wspec(w, layer, block, index_map, **kwargs):
    if w.ndim == 2:
        return pl.BlockSpec(block, index_map, **kwargs)
    return pl.BlockSpec((None,) + block, lambda *ids: (layer,) + index_map(*ids), **kwargs)


def _params(sem, vmem_limit_bytes=VMEM_LIMIT_BYTES):
    return pltpu.CompilerParams(dimension_semantics=sem, vmem_limit_bytes=vmem_limit_bytes)


def _sigmoid(z):
    return 1.0 / (1.0 + jnp.exp(-z))


def _softplus(z):
    return jnp.maximum(z, 0.0) + jnp.log1p(jnp.exp(-jnp.abs(z)))


def _gelu_tanh(z):
    c = math.sqrt(2.0 / math.pi)
    return 0.5 * z * (1.0 + jnp.tanh(c * (z + 0.044715 * (z * z * z))))


def _rms(y, g):
    ms = jnp.mean(y * y, axis=-1, keepdims=True)
    return y * lax.rsqrt(ms + EPS) * g


COL_CHUNK = 1024


def _col_chunks(d):
    step = min(COL_CHUNK, d)
    return [(c, min(step, d - c)) for c in range(0, d, step)]


def _inv_rms(load_chunk, d):
    ss = None
    for c, w in _col_chunks(d):
        y = load_chunk(c, w)
        part = jnp.sum(y * y, axis=-1, keepdims=True)
        ss = part if ss is None else ss + part
    return lax.rsqrt(ss * (1.0 / d) + EPS)


def _prenorm_to(xn_ref, x_ref, g_ref):
    d = x_ref.shape[1]
    inv = _inv_rms(lambda c, w: x_ref[:, c:c + w], d)
    for c, w in _col_chunks(d):
        xn_ref[:, c:c + w] = (x_ref[:, c:c + w] * inv * g_ref[:, c:c + w]).astype(xn_ref.dtype)


def _postnorm_residual_inplace(o_ref, xres_ref, g_ref):
    d = o_ref.shape[1]
    inv = _inv_rms(lambda c, w: o_ref[:, c:c + w], d)
    for c, w in _col_chunks(d):
        o_ref[:, c:c + w] = xres_ref[:, c:c + w] + o_ref[:, c:c + w] * inv * g_ref[:, c:c + w]


def _accumulate_dot(o_ref, a, w_ref):
    d = o_ref.shape[1]
    for c, w in _col_chunks(d):
        o_ref[:, c:c + w] += jnp.dot(a, w_ref[:, c:c + w], preferred_element_type=F32)


def _norm_mm_kernel(x_ref, g_ref, w_ref, b_ref, o_ref, xn_ref):
    @pl.when(pl.program_id(1) == 0)
    def _():
        _prenorm_to(xn_ref, x_ref, g_ref)

    o_ref[...] = jnp.dot(xn_ref[...], w_ref[...], preferred_element_type=F32) + b_ref[...]


def _norm_mm_side_kernel(x_ref, g_ref, w_ref, b_ref, ws_ref, o_ref, os_ref, xn_ref):
    @pl.when(pl.program_id(1) == 0)
    def _():
        _prenorm_to(xn_ref, x_ref, g_ref)
        os_ref[...] = jnp.dot(xn_ref[...], ws_ref[...], preferred_element_type=F32)

    o_ref[...] = jnp.dot(xn_ref[...], w_ref[...], preferred_element_type=F32) + b_ref[...]


def norm_matmul_with_side(x, g, w, b, w_side, *, tm_pref=512, tn_pref=1024):
    s, d = x.shape
    n = b.shape[0]
    ns = w_side.shape[1]
    tm, tn = _tile(s, tm_pref, SUBLANES), _tile(n, tn_pref)
    assert n % tn == 0 and n <= w.shape[1]
    return pl.pallas_call(
        _norm_mm_side_kernel,
        grid=(s // tm, n // tn),
        in_specs=[
            pl.BlockSpec((tm, d), lambda i, j: (i, 0)),
            pl.BlockSpec((1, d), lambda i, j: (0, 0)),
            pl.BlockSpec((d, tn), lambda i, j: (0, j)),
            pl.BlockSpec((1, tn), lambda i, j: (0, j)),
            pl.BlockSpec((d, ns), lambda i, j: (0, 0)),
        ],
        out_specs=[
            pl.BlockSpec((tm, tn), lambda i, j: (i, j)),
            pl.BlockSpec((tm, ns), lambda i, j: (i, 0)),
        ],
        out_shape=[jax.ShapeDtypeStruct((s, n), F32), jax.ShapeDtypeStruct((s, ns), F32)],
        scratch_shapes=[pltpu.VMEM((tm, d), BF16)],
        compiler_params=_params(("parallel", "arbitrary")),
        name="norm_matmul_with_side",
    )(x, g.reshape(1, d), w, b.reshape(1, n), w_side)


def norm_matmul(x, g, w, b, *, layer=0, tm_pref=512, tn_pref=1024):
    s, d = x.shape
    n = w.shape[-1]
    tm, tn = _tile(s, tm_pref, SUBLANES), _tile(n, tn_pref)
    return pl.pallas_call(
        _norm_mm_kernel,
        grid=(s // tm, n // tn),
        in_specs=[
            pl.BlockSpec((tm, d), lambda i, j: (i, 0)),
            pl.BlockSpec((1, d), lambda i, j: (0, 0)),
            _wspec(w, layer, (d, tn), lambda i, j: (0, j)),
            pl.BlockSpec((1, tn), lambda i, j: (0, j)),
        ],
        out_specs=pl.BlockSpec((tm, tn), lambda i, j: (i, j)),
        out_shape=jax.ShapeDtypeStruct((s, n), F32),
        scratch_shapes=[pltpu.VMEM((tm, d), BF16)],
        compiler_params=_params(("parallel", "arbitrary")),
        name="norm_matmul",
    )(x, g.reshape(1, d), w, b.reshape(1, n))


def _mm_postnorm_kernel(a_ref, w_ref, b_ref, xres_ref, g_ref, o_ref):
    k = pl.program_id(1)

    @pl.when(k == 0)
    def _():
        o_ref[...] = jnp.broadcast_to(b_ref[...], o_ref.shape)

    _accumulate_dot(o_ref, a_ref[...].astype(BF16), w_ref)

    @pl.when(k == pl.num_programs(1) - 1)
    def _():
        _postnorm_residual_inplace(o_ref, xres_ref, g_ref)


def _mm_postnorm_resident_kernel(a_ref, w_ref, b_ref, xres_ref, g_ref, o_ref):
    a = a_ref[...].astype(BF16)
    d = o_ref.shape[1]
    for c, w in _col_chunks(d):
        o_ref[:, c:c + w] = jnp.dot(a, w_ref[:, c:c + w], preferred_element_type=F32) + b_ref[:, c:c + w]
    _postnorm_residual_inplace(o_ref, xres_ref, g_ref)


RESIDENT_WEIGHT_MAX_BYTES = 32 * 1024 * 1024
RESIDENT_VMEM_LIMIT_BYTES = 60 * 1024 * 1024


def matmul_postnorm_residual(a, w, b, xres, g, *, layer=0, tm_pref=512, tk_pref=1024):
    s, kdim = a.shape
    d = w.shape[-1]
    if kdim * d * w.dtype.itemsize <= RESIDENT_WEIGHT_MAX_BYTES:
        tm = _tile(s, tm_pref // 2, SUBLANES)
        return pl.pallas_call(
            _mm_postnorm_resident_kernel,
            grid=(s // tm,),
            in_specs=[
                pl.BlockSpec((tm, kdim), lambda i: (i, 0)),
                _wspec(w, layer, (kdim, d), lambda i: (0, 0), pipeline_mode=pl.Buffered(1)),
                pl.BlockSpec((1, d), lambda i: (0, 0)),
                pl.BlockSpec((tm, d), lambda i: (i, 0)),
                pl.BlockSpec((1, d), lambda i: (0, 0)),
            ],
            out_specs=pl.BlockSpec((tm, d), lambda i: (i, 0)),
            out_shape=jax.ShapeDtypeStruct((s, d), F32),
            compiler_params=_params(("parallel",), RESIDENT_VMEM_LIMIT_BYTES),
            name="matmul_postnorm_residual_resident",
        )(a, w, b.reshape(1, d), xres, g.reshape(1, d))
    tm, tk = _tile(s, tm_pref, SUBLANES), _tile(kdim, tk_pref)
    return pl.pallas_call(
        _mm_postnorm_kernel,
        grid=(s // tm, kdim // tk),
        in_specs=[
            pl.BlockSpec((tm, tk), lambda i, k: (i, k)),
            _wspec(w, layer, (tk, d), lambda i, k: (k, 0)),
            pl.BlockSpec((1, d), lambda i, k: (0, 0)),
            pl.BlockSpec((tm, d), lambda i, k: (i, 0)),
            pl.BlockSpec((1, d), lambda i, k: (0, 0)),
        ],
        out_specs=pl.BlockSpec((tm, d), lambda i, k: (i, 0)),
        out_shape=jax.ShapeDtypeStruct((s, d), F32),
        compiler_params=_params(("parallel", "arbitrary")),
        name="matmul_postnorm_residual",
    )(a, w, b.reshape(1, d), xres, g.reshape(1, d))


def _ffn_kernel(x_ref, gpre_ref, wg_ref, wu_ref, wo_ref, gpost_ref, o_ref, xn_ref):
    f = pl.program_id(1)

    @pl.when(f == 0)
    def _():
        _prenorm_to(xn_ref, x_ref, gpre_ref)
        o_ref[...] = jnp.zeros_like(o_ref)

    xn = xn_ref[...]
    gate = jnp.dot(xn, wg_ref[...], preferred_element_type=F32)
    up = jnp.dot(xn, wu_ref[...], preferred_element_type=F32)
    act = (gate * _sigmoid(gate) * up).astype(BF16)
    _accumulate_dot(o_ref, act, wo_ref)

    @pl.when(f == pl.num_programs(1) - 1)
    def _():
        _postnorm_residual_inplace(o_ref, x_ref, gpost_ref)


def ffn(x, g_pre, w_in, w_out, g_post, *, layer=0, tm_pref=512, tf_pref=256):
    s, d = x.shape
    dff = w_out.shape[-2]
    tm, tf = _tile(s, tm_pref, SUBLANES), _tile(dff, tf_pref)
    nf = dff // tf
    return pl.pallas_call(
        _ffn_kernel,
        grid=(s // tm, nf),
        in_specs=[
            pl.BlockSpec((tm, d), lambda i, f: (i, 0)),
            pl.BlockSpec((1, d), lambda i, f: (0, 0)),
            _wspec(w_in, layer, (d, tf), lambda i, f: (0, f)),
            _wspec(w_in, layer, (d, tf), lambda i, f: (0, f + nf)),
            _wspec(w_out, layer, (tf, d), lambda i, f: (f, 0)),
            pl.BlockSpec((1, d), lambda i, f: (0, 0)),
        ],
        out_specs=pl.BlockSpec((tm, d), lambda i, f: (i, 0)),
        out_shape=jax.ShapeDtypeStruct((s, d), F32),
        scratch_shapes=[pltpu.VMEM((tm, d), BF16)],
        compiler_params=_params(("parallel", "arbitrary")),
        name="ffn",
    )(x, g_pre.reshape(1, d), w_in, w_in, w_out, g_post.reshape(1, d))


CONV_HALO = 32
CONV_ROWS = 32
GLU_CONV_STAGES = 8
MXU_K_TILE = 256


def _anchor_zero(v):
    u = pltpu.bitcast(v, jnp.uint32)
    z = lax.shift_right_logical(lax.shift_right_logical(u, jnp.uint32(16)), jnp.uint32(16))
    return pltpu.bitcast(z, F32)


def _glu_conv_kernel(x_ref, g_ref, wv_ref, wg_ref, bv_ref, bg_ref, dw_ref, dwb_ref, y_ref,
                     xn_ref, win_ref, carry_ref, *, nj):
    s = pl.program_id(0)
    total = pl.num_programs(0) - 1
    tm = x_ref.shape[0]
    tn = wv_ref.shape[1]
    kw = dw_ref.shape[0]
    lane_tiles = [slice(c, c + LANES) for c in range(0, tn, LANES)]

    @pl.when(s == 0)
    def _():
        win_ref[...] = jnp.zeros_like(win_ref)
        carry_ref[...] = jnp.zeros_like(carry_ref)

    @pl.when(jnp.logical_and(s % nj == 0, s < total))
    def _():
        _prenorm_to(xn_ref, x_ref, g_ref)

    jc = jnp.maximum(s - 1, 0) % nj
    win_ref[:, 0:CONV_HALO, :] = carry_ref[jc]
    base = CONV_HALO - (kw - 1)
    rows = min(CONV_ROWS, tm)
    row_starts = list(range(0, tm, rows))
    d = xn_ref.shape[1]
    stages = max(1, min(GLU_CONV_STAGES, len(row_starts), d // MXU_K_TILE))
    assert len(row_starts) % stages == 0 and d % stages == 0
    per_stage = len(row_starts) // stages
    kc = d // stages
    val = gate = None
    conv_done = dot_done = None
    for g in range(stages):
        xk = xn_ref[:, g * kc:(g + 1) * kc]
        if conv_done is not None:
            xk = xk + _anchor_zero(conv_done)[0:1, 0:1].astype(xk.dtype)
        pv = jnp.dot(xk, wv_ref[g * kc:(g + 1) * kc, :], preferred_element_type=F32)
        pg = jnp.dot(xk, wg_ref[g * kc:(g + 1) * kc, :], preferred_element_type=F32)
        val = pv if val is None else val + pv
        gate = pg if gate is None else gate + pg

        prev = dot_done
        for r0 in row_starts[g * per_stage:(g + 1) * per_stage]:
            for c, sl in enumerate(lane_tiles):
                acc = jnp.broadcast_to(dwb_ref[:, sl], (rows, LANES))
                if prev is not None:
                    acc = acc + _anchor_zero(prev)[0:1, :]
                for k in range(kw):
                    acc = acc + dw_ref[k:k + 1, sl] * win_ref[c, base + r0 + k:base + r0 + k + rows, :]
                y_ref[r0:r0 + rows, sl] = acc
                prev = acc[0:SUBLANES, :]
        conv_done = prev
        dot_done = pg[tm - SUBLANES:tm, tn - LANES:tn]
    carry_ref[jc] = win_ref[:, tm:tm + CONV_HALO, :]

    h = (val + bv_ref[...]) * _sigmoid(gate + bg_ref[...])
    for c, sl in enumerate(lane_tiles):
        win_ref[c, CONV_HALO:CONV_HALO + tm, :] = h[:, sl]


def glu_dwconv(x, g, w, b, dw, dw_b, *, layer=0, tm_pref=512, tn_pref=512):
    s, d = x.shape
    n = w.shape[-1] // 2
    kw = dw.shape[0]
    tm, tn = _tile(s, tm_pref, CONV_HALO), _tile(n, tn_pref)
    assert kw - 1 <= CONV_HALO <= tm and tn % LANES == 0
    ni, nj = s // tm, n // tn
    total = ni * nj
    b2 = b.reshape(1, 2 * n)

    def proj(step):
        t = jnp.minimum(step, total - 1)
        return t // nj, t % nj

    def conv(step):
        t = jnp.maximum(step - 1, 0)
        return t // nj, t % nj

    return pl.pallas_call(
        functools.partial(_glu_conv_kernel, nj=nj),
        grid=(total + 1,),
        in_specs=[
            pl.BlockSpec((tm, d), lambda q: (proj(q)[0], 0)),
            pl.BlockSpec((1, d), lambda q: (0, 0)),
            _wspec(w, layer, (d, tn), lambda q: (0, proj(q)[1])),
            _wspec(w, layer, (d, tn), lambda q: (0, proj(q)[1] + nj)),
            pl.BlockSpec((1, tn), lambda q: (0, proj(q)[1])),
            pl.BlockSpec((1, tn), lambda q: (0, proj(q)[1] + nj)),
            pl.BlockSpec((kw, tn), lambda q: (0, conv(q)[1])),
            pl.BlockSpec((1, tn), lambda q: (0, conv(q)[1])),
        ],
        out_specs=pl.BlockSpec((tm, tn), lambda q: conv(q)),
        out_shape=jax.ShapeDtypeStruct((s, n), F32),
        scratch_shapes=[
            pltpu.VMEM((tm, d), BF16),
            pltpu.VMEM((tn // LANES, CONV_HALO + tm, LANES), F32),
            pltpu.VMEM((nj, tn // LANES, CONV_HALO, LANES), F32),
        ],
        compiler_params=_params(("arbitrary",)),
        name="glu_dwconv",
    )(x, g.reshape(1, d), w, w, b2, b2, dw, dw_b.reshape(1, n))


def _ln_silu_kernel(y_ref, lng_ref, lnb_ref, o_ref):
    y = y_ref[...]
    mu = jnp.mean(y, axis=-1, keepdims=True)
    yc = y - mu
    var = jnp.mean(yc * yc, axis=-1, keepdims=True)
    z = yc * lax.rsqrt(var + EPS) * lng_ref[...] + lnb_ref[...]
    o_ref[...] = (z * _sigmoid(z)).astype(o_ref.dtype)


def ln_silu(y, ln_g, ln_b, *, t_pref=256):
    s, d = y.shape
    t = _tile(s, t_pref, SUBLANES)
    return pl.pallas_call(
        _ln_silu_kernel,
        grid=(s // t,),
        in_specs=[
            pl.BlockSpec((t, d), lambda i: (i, 0)),
            pl.BlockSpec((1, d), lambda i: (0, 0)),
            pl.BlockSpec((1, d), lambda i: (0, 0)),
        ],
        out_specs=pl.BlockSpec((t, d), lambda i: (i, 0)),
        out_shape=jax.ShapeDtypeStruct((s, d), BF16),
        compiler_params=_params(("parallel",)),
        name="ln_silu",
    )(y, ln_g.reshape(1, d), ln_b.reshape(1, d))


LRU_HALO = SUBLANES


def _lru_kernel(recprev_ref, rec_ref, gb_ref, cw_ref, cb_ref, wa_ref, wx_ref, ba_ref, bx_ref,
                lam_ref, y_ref, win_ref, a_ref, b_ref, h_ref, carry_ref):
    i = pl.program_id(1)
    t, c = rec_ref.shape
    kw = cw_ref.shape[0]

    @pl.when(i == 0)
    def _():
        carry_ref[...] = jnp.zeros_like(carry_ref)

    keep = (i > 0).astype(F32)
    base = LRU_HALO - (kw - 1)
    pieces = []
    for ci, c0 in enumerate(range(0, c, LANES)):
        sl = slice(c0, c0 + LANES)
        win_ref[ci, 0:LRU_HALO, :] = recprev_ref[:, sl] * keep
        win_ref[ci, LRU_HALO:LRU_HALO + t, :] = rec_ref[:, sl]
        piece = jnp.broadcast_to(cb_ref[:, sl], (t, LANES))
        for k in range(kw):
            piece = piece + cw_ref[k:k + 1, sl] * win_ref[ci, base + k:base + k + t, :]
        pieces.append(piece)
    u = jnp.concatenate(pieces, axis=1)

    ub = u.astype(BF16)
    r = _sigmoid(jnp.dot(ub, wa_ref[0], preferred_element_type=F32) + ba_ref[...])
    ig = _sigmoid(jnp.dot(ub, wx_ref[0], preferred_element_type=F32) + bx_ref[...])
    log_a = (-LRU_C) * r * _softplus(-lam_ref[...])
    a_ref[...] = jnp.exp(log_a)
    b_ref[...] = jnp.sqrt(1.0 - jnp.exp(2.0 * log_a)) * (ig * u)

    def step(tt, h):
        h = a_ref[pl.ds(tt, 1), :] * h + b_ref[pl.ds(tt, 1), :]
        h_ref[pl.ds(tt, 1), :] = h
        return h

    carry_ref[...] = lax.fori_loop(0, t, step, carry_ref[...], unroll=8)
    y_ref[...] = (h_ref[...] * _gelu_tanh(gb_ref[...])).astype(y_ref.dtype)


def lru_core(hin, conv_w, conv_b, wa, wx, ba, bx, lam, *, t_pref=256):
    s = hin.shape[0]
    r = hin.shape[1] // 2
    ng, c, _ = wa.shape
    kw = conv_w.shape[0]
    assert kw - 1 <= LRU_HALO and ng * c == r
    t = _tile(s, t_pref, LRU_HALO)
    ratio = t // LRU_HALO
    row = lambda v: v.reshape(1, r)
    vec_spec = pl.BlockSpec((1, c), lambda g, i: (0, g))
    return pl.pallas_call(
        _lru_kernel,
        grid=(ng, s // t),
        in_specs=[
            pl.BlockSpec((LRU_HALO, c), lambda g, i: (jnp.maximum(i * ratio - 1, 0), ng + g)),
            pl.BlockSpec((t, c), lambda g, i: (i, ng + g)),
            pl.BlockSpec((t, c), lambda g, i: (i, g)),
            pl.BlockSpec((kw, c), lambda g, i: (0, g)),
            vec_spec,
            pl.BlockSpec((1, c, c), lambda g, i: (g, 0, 0)),
            pl.BlockSpec((1, c, c), lambda g, i: (g, 0, 0)),
            vec_spec, vec_spec, vec_spec,
        ],
        out_specs=pl.BlockSpec((t, c), lambda g, i: (i, g)),
        out_shape=jax.ShapeDtypeStruct((s, r), BF16),
        scratch_shapes=[
            pltpu.VMEM((c // LANES, LRU_HALO + t, LANES), F32),
            pltpu.VMEM((t, c), F32),
            pltpu.VMEM((t, c), F32),
            pltpu.VMEM((t, c), F32),
            pltpu.VMEM((1, c), F32),
        ],
        compiler_params=_params(("parallel", "arbitrary")),
        name="lru_core",
    )(hin, hin, hin, conv_w, row(conv_b), wa, wx, row(ba), row(bx), row(lam))


def _block_diag_groups(w, heads_per_group):
    h, hd, _ = w.shape
    g = heads_per_group
    wg = w.reshape(h // g, g, hd, hd)
    eye = jnp.eye(g, dtype=w.dtype)
    return jnp.einsum('ngij,gh->ngihj', wg, eye).reshape(h // g, g * hd, g * hd)


def _heads_per_group(n_heads, head_dim):
    g = 1
    while (g * head_dim) % LANES != 0 and g < n_heads:
        g += 1
    assert n_heads % g == 0
    return g


def _mlstm_kernel(q_ref, k_ref, v_ref, o_ref, gates_ref, gbias_ref, hg_ref, out_ref,
                  c_ref, n_ref, m_ref, *, n_heads):
    ci = pl.program_id(0)
    L = q_ref.shape[0]
    dqk = q_ref.shape[1] // n_heads
    dv = v_ref.shape[1] // n_heads
    scale = dqk ** -0.5

    @pl.when(ci == 0)
    def _():
        c_ref[...] = jnp.zeros_like(c_ref)
        n_ref[...] = jnp.zeros_like(n_ref)
        m_ref[...] = jnp.zeros_like(m_ref)

    gz = gates_ref[...] + gbias_ref[...]
    lane = lax.broadcasted_iota(jnp.int32, gz.shape, 1)
    glog = jnp.where(lane < n_heads, gz, -_softplus(-gz))
    glog_t = glog.T

    rowi = lax.broadcasted_iota(jnp.int32, (L, L), 0)
    coli = lax.broadcasted_iota(jnp.int32, (L, L), 1)
    causal = coli <= rowi

    for h in range(n_heads):
        li_col = glog[:, h:h + 1]
        li_row = glog_t[h:h + 1, :]
        lf_col = glog[:, n_heads + h:n_heads + h + 1]
        lf_row = glog_t[n_heads + h:n_heads + h + 1, :]
        b_col = jnp.sum(jnp.where(causal, lf_row, 0.0), axis=1, keepdims=True)
        b_row = jnp.sum(jnp.where(rowi <= coli, lf_col, 0.0), axis=0, keepdims=True)
        m_st = m_ref[h][0:1, 0:1]
        d_mat = jnp.where(causal, b_col - b_row + li_row, -jnp.inf)
        m_inter = b_col + m_st
        m_out = jnp.maximum(m_inter, jnp.max(d_mat, axis=1, keepdims=True))
        w_inter = jnp.exp(m_inter - m_out)

        qh = q_ref[:, h * dqk:(h + 1) * dqk]
        kh = k_ref[:, h * dqk:(h + 1) * dqk] * scale
        vh = v_ref[:, h * dv:(h + 1) * dv]
        qb, kb, vb = qh.astype(BF16), kh.astype(BF16), vh.astype(BF16)

        qk = lax.dot_general(qb, kb, (((1,), (1,)), ((), ())), preferred_element_type=F32)
        s_mat = qk * jnp.exp(d_mat - m_out)
        c_st = c_ref[h]
        n_st = n_ref[h]
        num = w_inter * jnp.dot(qb, c_st.astype(BF16), preferred_element_type=F32) \
            + jnp.dot(s_mat.astype(BF16), vb, preferred_element_type=F32)
        den = w_inter * jnp.sum(qh * n_st, axis=1, keepdims=True) + jnp.sum(s_mat, axis=1, keepdims=True)
        hh = num / jnp.maximum(jnp.abs(den), jnp.exp(-m_out))

        b_last = b_col[L - 1:L, :]
        a_s = b_last - b_col + li_col
        m_new = jnp.maximum(b_last + m_st, jnp.max(a_s, axis=0, keepdims=True))
        w_old = jnp.exp(b_last + m_st - m_new)
        w_s = jnp.exp(a_s - m_new)
        kw_ = kh * w_s
        c_ref[h] = w_old * c_st + lax.dot_general(kw_.astype(BF16), vb, (((0,), (0,)), ((), ())),
                                                  preferred_element_type=F32)
        n_ref[h] = w_old * n_st + jnp.sum(kw_, axis=0, keepdims=True)
        m_ref[h] = jnp.broadcast_to(m_new, m_ref.shape[1:])

        hn = hh * lax.rsqrt(jnp.mean(hh * hh, axis=-1, keepdims=True) + EPS) * hg_ref[:, h * dv:(h + 1) * dv]
        out_ref[:, h * dv:(h + 1) * dv] = (hn * _sigmoid(o_ref[:, h * dv:(h + 1) * dv])).astype(out_ref.dtype)


def mlstm_core(p, gates, gate_bias_row, head_g, *, n_heads, qk, v):
    s = p.shape[0]
    L = _tile(s, MLSTM_CHUNK, SUBLANES)
    assert (2 * qk) % v == 0
    voff = (2 * qk) // v
    dqk, dv = qk // n_heads, v // n_heads
    return pl.pallas_call(
        functools.partial(_mlstm_kernel, n_heads=n_heads),
        grid=(s // L,),
        in_specs=[
            pl.BlockSpec((L, qk), lambda c: (c, 0)),
            pl.BlockSpec((L, qk), lambda c: (c, 1)),
            pl.BlockSpec((L, v), lambda c: (c, voff)),
            pl.BlockSpec((L, v), lambda c: (c, voff + 1)),
            pl.BlockSpec((L, LANES), lambda c: (c, 0)),
            pl.BlockSpec((1, LANES), lambda c: (0, 0)),
            pl.BlockSpec((1, v), lambda c: (0, 0)),
        ],
        out_specs=pl.BlockSpec((L, v), lambda c: (c, 0)),
        out_shape=jax.ShapeDtypeStruct((s, v), BF16),
        scratch_shapes=[
            pltpu.VMEM((n_heads, dqk, dv), F32),
            pltpu.VMEM((n_heads, 1, dqk), F32),
            pltpu.VMEM((n_heads, SUBLANES, LANES), F32),
        ],
        compiler_params=_params(("arbitrary",)),
        name="mlstm_core",
    )(p, p, p, p, gates, gate_bias_row, head_g.reshape(1, v))


def _conv_layer(x, g_pre, g_post, layer, w_in, b_in, dw, dw_b, ln_g, ln_b, w_out, b_out):
    y = glu_dwconv(x, g_pre, w_in, b_in, dw, dw_b, layer=layer)
    a = ln_silu(y, ln_g, ln_b)
    return matmul_postnorm_residual(a, w_out, b_out, x, g_post, layer=layer)


def _lru_layer(x, g_pre, g_post, layer, w_in, b_in, conv_w, conv_b, gate_a_w, gate_x_w, gate_a_b, gate_x_b,
               lam, w_out, b_out):
    n_heads, head_dim, _ = gate_a_w.shape
    hpg = _heads_per_group(n_heads, head_dim)
    wa = _block_diag_groups(gate_a_w, hpg).astype(BF16)
    wx = _block_diag_groups(gate_x_w, hpg).astype(BF16)
    hin = norm_matmul(x, g_pre, w_in, b_in, layer=layer)
    y = lru_core(hin, conv_w, conv_b, wa, wx, gate_a_b, gate_x_b, lam)
    return matmul_postnorm_residual(y, w_out, b_out, x, g_post, layer=layer)


def _mlstm_layer(x, g_pre, g_post, w_in, gate_b, head_g, w_out):
    d = x.shape[1]
    n_heads = gate_b.shape[1]
    v = w_out.shape[0]
    proj = w_in.shape[1]
    qk = (proj - 2 * v - 2 * n_heads) // 2
    main = 2 * qk + 2 * v
    assert 2 * n_heads <= LANES
    w_bf = w_in.astype(BF16)
    w_gate = jnp.pad(w_bf[:, main:], ((0, 0), (0, LANES - 2 * n_heads)))
    p, gates = norm_matmul_with_side(x, g_pre, w_bf, jnp.zeros((main,), F32), w_gate)
    gbias = jnp.pad(gate_b.reshape(1, 2 * n_heads), ((0, 0), (0, LANES - 2 * n_heads)))
    hcore = mlstm_core(p, gates, gbias, head_g, n_heads=n_heads, qk=qk, v=v)
    return matmul_postnorm_residual(hcore, w_out.astype(BF16), jnp.zeros((d,), F32), x, g_post)


def kernel(x, norm_g, ffn_w_in, ffn_w_out, conv_w_in, conv_b_in, conv_dw, conv_dw_b, conv_ln_g, conv_ln_b, conv_w_out, conv_b_out, lru_w_in, lru_b_in, lru_conv_w, lru_conv_b, lru_gate_a_w, lru_gate_x_w, lru_gate_a_b, lru_gate_x_b, lru_lambda, lru_w_out, lru_b_out, mlstm_w_in, mlstm_gate_b, mlstm_head_g, mlstm_w_out):
    bsz, seq, d = x.shape
    assert bsz == 1, "sequence mixers are written for a single sequence"
    xs = x.reshape(seq, d)
    ffn_in, ffn_out = ffn_w_in.astype(BF16), ffn_w_out.astype(BF16)
    conv_in, conv_out = conv_w_in.astype(BF16), conv_w_out.astype(BF16)
    lru_in, lru_out = lru_w_in.astype(BF16), lru_w_out.astype(BF16)
    ia = ib = ic = 0
    for i in range(norm_g.shape[0]):
        kind = i % N_MIXERS
        g = norm_g[i]
        if kind == 0:
            xs = _conv_layer(xs, g[0], g[1], ia, conv_in, conv_b_in[ia], conv_dw[ia], conv_dw_b[ia],
                             conv_ln_g[ia], conv_ln_b[ia], conv_out, conv_b_out[ia])
            ia += 1
        elif kind == 1:
            xs = _lru_layer(xs, g[0], g[1], ib, lru_in, lru_b_in[ib], lru_conv_w[ib], lru_conv_b[ib],
                            lru_gate_a_w[ib], lru_gate_x_w[ib], lru_gate_a_b[ib], lru_gate_x_b[ib],
                            lru_lambda[ib], lru_out, lru_b_out[ib])
            ib += 1
        else:
            xs = _mlstm_layer(xs, g[0], g[1], mlstm_w_in[ic], mlstm_gate_b[ic], mlstm_head_g[ic],
                              mlstm_w_out[ic])
            ic += 1
        xs = ffn(xs, g[2], ffn_in, ffn_out, g[3], layer=i)
    return xs.reshape(bsz, seq, d)
```

```python
import functools
import math

import jax
import jax.numpy as jnp
from jax import lax
from jax.experimental import pallas as pl
from jax.experimental.pallas import tpu as pltpu

EPS = 1e-6
LRU_C = 8.0
N_MIXERS = 3
MLSTM_CHUNK = 256
LANES = 128
SUBLANES = 8
VMEM_LIMIT_BYTES = 56 * 1024 * 1024

F32 = jnp.float32
BF16 = jnp.bfloat16


def _tile(n, pref, align=LANES):
    if n <= pref:
        return n
    t = (pref // align) * align
    while t >= align:
        if n % t == 0:
            return t
        t -= align
    return n


def _wspec(w, layer, block, index_map, **kwargs):
    if w.ndim == 2:
        return pl.BlockSpec(block, index_map, **kwargs)
    return pl.BlockSpec((None,) + block, lambda *ids: (layer,) + index_map(*ids), **kwargs)


def _params(sem, vmem_limit_bytes=VMEM_LIMIT_BYTES):
    return pltpu.CompilerParams(dimension_semantics=sem, vmem_limit_bytes=vmem_limit_bytes)


def _sigmoid(z):
    return 1.0 / (1.0 + jnp.exp(-z))


def _softplus(z):
    return jnp.maximum(z, 0.0) + jnp.log1p(jnp.exp(-jnp.abs(z)))


def _gelu_tanh(z):
    c = math.sqrt(2.0 / math.pi)
    return 0.5 * z * (1.0 + jnp.tanh(c * (z + 0.044715 * (z * z * z))))


COL_CHUNK = 1024


def _col_chunks(d):
    step = min(COL_CHUNK, d)
    return [(c, min(step, d - c)) for c in range(0, d, step)]


def _inv_rms(load_chunk, d):
    ss = None
    for c, w in _col_chunks(d):
        y = load_chunk(c, w)
        part = jnp.sum(y * y, axis=-1, keepdims=True)
        ss = part if ss is None else ss + part
    return lax.rsqrt(ss * (1.0 / d) + EPS)


def _prenorm_to(xn_ref, x_ref, g_ref):
    d = x_ref.shape[1]
    inv = _inv_rms(lambda c, w: x_ref[:, c:c + w], d)
    for c, w in _col_chunks(d):
        xn_ref[:, c:c + w] = (x_ref[:, c:c + w] * inv * g_ref[:, c:c + w]).astype(xn_ref.dtype)


def _postnorm_residual_inplace(o_ref, xres_ref, g_ref):
    d = o_ref.shape[1]
    inv = _inv_rms(lambda c, w: o_ref[:, c:c + w], d)
    for c, w in _col_chunks(d):
        o_ref[:, c:c + w] = xres_ref[:, c:c + w] + o_ref[:, c:c + w] * inv * g_ref[:, c:c + w]


def _accumulate_dot(o_ref, a, w_ref):
    d = o_ref.shape[1]
    for c, w in _col_chunks(d):
        o_ref[:, c:c + w] += jnp.dot(a, w_ref[:, c:c + w], preferred_element_type=F32)


def _norm_mm_kernel(x_ref, g_ref, w_ref, b_ref, o_ref, xn_ref):
    @pl.when(pl.program_id(1) == 0)
    def _():
        _prenorm_to(xn_ref, x_ref, g_ref)

    o_ref[...] = jnp.dot(xn_ref[...], w_ref[...], preferred_element_type=F32) + b_ref[...]


def _norm_mm_side_kernel(x_ref, g_ref, w_ref, b_ref, ws_ref, o_ref, os_ref, xn_ref):
    @pl.when(pl.program_id(1) == 0)
    def _():
        _prenorm_to(xn_ref, x_ref, g_ref)
        os_ref[...] = jnp.dot(xn_ref[...], ws_ref[...], preferred_element_type=F32)

    o_ref[...] = jnp.dot(xn_ref[...], w_ref[...], preferred_element_type=F32) + b_ref[...]


def norm_matmul_with_side(x, g, w, b, w_side, *, tm_pref=512, tn_pref=1024):
    s, d = x.shape
    n = b.shape[0]
    ns = w_side.shape[1]
    tm, tn = _tile(s, tm_pref, SUBLANES), _tile(n, tn_pref)
    assert n % tn == 0 and n <= w.shape[1]
    return pl.pallas_call(
        _norm_mm_side_kernel,
        grid=(s // tm, n // tn),
        in_specs=[
            pl.BlockSpec((tm, d), lambda i, j: (i, 0)),
            pl.BlockSpec((1, d), lambda i, j: (0, 0)),
            pl.BlockSpec((d, tn), lambda i, j: (0, j)),
            pl.BlockSpec((1, tn), lambda i, j: (0, j)),
            pl.BlockSpec((d, ns), lambda i, j: (0, 0)),
        ],
        out_specs=[
            pl.BlockSpec((tm, tn), lambda i, j: (i, j)),
            pl.BlockSpec((tm, ns), lambda i, j: (i, 0)),
        ],
        out_shape=[jax.ShapeDtypeStruct((s, n), F32), jax.ShapeDtypeStruct((s, ns), F32)],
        scratch_shapes=[pltpu.VMEM((tm, d), BF16)],
        compiler_params=_params(("parallel", "arbitrary")),
        name="norm_matmul_with_side",
    )(x, g.reshape(1, d), w, b.reshape(1, n), w_side)


def norm_matmul(x, g, w, b, *, layer=0, tm_pref=512, tn_pref=1024):
    s, d = x.shape
    n = w.shape[-1]
    tm, tn = _tile(s, tm_pref, SUBLANES), _tile(n, tn_pref)
    return pl.pallas_call(
        _norm_mm_kernel,
        grid=(s // tm, n // tn),
        in_specs=[
            pl.BlockSpec((tm, d), lambda i, j: (i, 0)),
            pl.BlockSpec((1, d), lambda i, j: (0, 0)),
            _wspec(w, layer, (d, tn), lambda i, j: (0, j)),
            pl.BlockSpec((1, tn), lambda i, j: (0, j)),
        ],
        out_specs=pl.BlockSpec((tm, tn), lambda i, j: (i, j)),
        out_shape=jax.ShapeDtypeStruct((s, n), F32),
        scratch_shapes=[pltpu.VMEM((tm, d), BF16)],
        compiler_params=_params(("parallel", "arbitrary")),
        name="norm_matmul",
    )(x, g.reshape(1, d), w, b.reshape(1, n))


def _mm_postnorm_kernel(a_ref, w_ref, b_ref, xres_ref, g_ref, o_ref):
    k = pl.program_id(1)

    @pl.when(k == 0)
    def _():
        o_ref[...] = jnp.broadcast_to(b_ref[...], o_ref.shape)

    _accumulate_dot(o_ref, a_ref[...].astype(BF16), w_ref)

    @pl.when(k == pl.num_programs(1) - 1)
    def _():
        _postnorm_residual_inplace(o_ref, xres_ref, g_ref)


def _mm_postnorm_resident_kernel(a_ref, w_ref, b_ref, xres_ref, g_ref, o_ref):
    a = a_ref[...].astype(BF16)
    d = o_ref.shape[1]
    for c, w in _col_chunks(d):
        o_ref[:, c:c + w] = jnp.dot(a, w_ref[:, c:c + w], preferred_element_type=F32) + b_ref[:, c:c + w]
    _postnorm_residual_inplace(o_ref, xres_ref, g_ref)


RESIDENT_WEIGHT_MAX_BYTES = 32 * 1024 * 1024
RESIDENT_VMEM_LIMIT_BYTES = 60 * 1024 * 1024


def matmul_postnorm_residual(a, w, b, xres, g, *, layer=0, tm_pref=512, tk_pref=1024):
    s, kdim = a.shape
    d = w.shape[-1]
    if kdim * d * w.dtype.itemsize <= RESIDENT_WEIGHT_MAX_BYTES:
        tm = _tile(s, tm_pref // 2, SUBLANES)
        return pl.pallas_call(
            _mm_postnorm_resident_kernel,
            grid=(s // tm,),
            in_specs=[
                pl.BlockSpec((tm, kdim), lambda i: (i, 0)),
                _wspec(w, layer, (kdim, d), lambda i: (0, 0), pipeline_mode=pl.Buffered(1)),
                pl.BlockSpec((1, d), lambda i: (0, 0)),
                pl.BlockSpec((tm, d), lambda i: (i, 0)),
                pl.BlockSpec((1, d), lambda i: (0, 0)),
            ],
            out_specs=pl.BlockSpec((tm, d), lambda i: (i, 0)),
            out_shape=jax.ShapeDtypeStruct((s, d), F32),
            compiler_params=_params(("parallel",), RESIDENT_VMEM_LIMIT_BYTES),
            name="matmul_postnorm_residual_resident",
        )(a, w, b.reshape(1, d), xres, g.reshape(1, d))
    tm, tk = _tile(s, tm_pref, SUBLANES), _tile(kdim, tk_pref)
    return pl.pallas_call(
        _mm_postnorm_kernel,
        grid=(s // tm, kdim // tk),
        in_specs=[
            pl.BlockSpec((tm, tk), lambda i, k: (i, k)),
            _wspec(w, layer, (tk, d), lambda i, k: (k, 0)),
            pl.BlockSpec((1, d), lambda i, k: (0, 0)),
            pl.BlockSpec((tm, d), lambda i, k: (i, 0)),
            pl.BlockSpec((1, d), lambda i, k: (0, 0)),
        ],
        out_specs=pl.BlockSpec((tm, d), lambda i, k: (i, 0)),
        out_shape=jax.ShapeDtypeStruct((s, d), F32),
        compiler_params=_params(("parallel", "arbitrary")),
        name="matmul_postnorm_residual",
    )(a, w, b.reshape(1, d), xres, g.reshape(1, d))


def _ffn_kernel(x_ref, gpre_ref, wg_ref, wu_ref, wo_ref, gpost_ref, o_ref, xn_ref):
    f = pl.program_id(1)

    @pl.when(f == 0)
    def _():
        _prenorm_to(xn_ref, x_ref, gpre_ref)
        o_ref[...] = jnp.zeros_like(o_ref)

    xn = xn_ref[...]
    gate = jnp.dot(xn, wg_ref[...], preferred_element_type=F32)
    up = jnp.dot(xn, wu_ref[...], preferred_element_type=F32)
    act = (gate * _sigmoid(gate) * up).astype(BF16)
    _accumulate_dot(o_ref, act, wo_ref)

    @pl.when(f == pl.num_programs(1) - 1)
    def _():
        _postnorm_residual_inplace(o_ref, x_ref, gpost_ref)


def ffn(x, g_pre, w_in, w_out, g_post, *, layer=0, tm_pref=512, tf_pref=256):
    s, d = x.shape
    dff = w_out.shape[-2]
    tm, tf = _tile(s, tm_pref, SUBLANES), _tile(dff, tf_pref)
    nf = dff // tf
    return pl.pallas_call(
        _ffn_kernel,
        grid=(s // tm, nf),
        in_specs=[
            pl.BlockSpec((tm, d), lambda i, f: (i, 0)),
            pl.BlockSpec((1, d), lambda i, f: (0, 0)),
            _wspec(w_in, layer, (d, tf), lambda i, f: (0, f)),
            _wspec(w_in, layer, (d, tf), lambda i, f: (0, f + nf)),
            _wspec(w_out, layer, (tf, d), lambda i, f: (f, 0)),
            pl.BlockSpec((1, d), lambda i, f: (0, 0)),
        ],
        out_specs=pl.BlockSpec((tm, d), lambda i, f: (i, 0)),
        out_shape=jax.ShapeDtypeStruct((s, d), F32),
        scratch_shapes=[pltpu.VMEM((tm, d), BF16)],
        compiler_params=_params(("parallel", "arbitrary")),
        name="ffn",
    )(x, g_pre.reshape(1, d), w_in, w_in, w_out, g_post.reshape(1, d))


CONV_HALO = 32
CONV_ROWS = 32
GLU_CONV_STAGES = 16
MXU_K_TILE = 256


def _anchor_zero(v):
    u = pltpu.bitcast(v, jnp.uint32)
    z = lax.shift_right_logical(lax.shift_right_logical(u, jnp.uint32(16)), jnp.uint32(16))
    return pltpu.bitcast(z, F32)


def _glu_conv_kernel(x_ref, g_ref, wv_ref, wg_ref, bv_ref, bg_ref, dw_ref, dwb_ref, y_ref,
                     xn_ref, win_ref, carry_ref, *, nj):
    s = pl.program_id(0)
    total = pl.num_programs(0) - 1
    tm = x_ref.shape[0]
    tn = wv_ref.shape[1]
    kw = dw_ref.shape[0]
    lane_tiles = [slice(c, c + LANES) for c in range(0, tn, LANES)]

    @pl.when(s == 0)
    def _():
        win_ref[...] = jnp.zeros_like(win_ref)
        carry_ref[...] = jnp.zeros_like(carry_ref)

    @pl.when(jnp.logical_and(s % nj == 0, s < total))
    def _():
        _prenorm_to(xn_ref, x_ref, g_ref)

    jc = jnp.maximum(s - 1, 0) % nj
    win_ref[:, 0:CONV_HALO, :] = carry_ref[jc]
    base = CONV_HALO - (kw - 1)
    rows = min(CONV_ROWS, tm)
    row_starts = list(range(0, tm, rows))
    d = xn_ref.shape[1]
    stages = max(1, min(GLU_CONV_STAGES, len(row_starts), d // MXU_K_TILE))
    assert len(row_starts) % stages == 0 and d % stages == 0
    per_stage = len(row_starts) // stages
    kc = d // stages
    val = gate = None
    conv_done = dot_done = None
    for g in range(stages):
        xk = xn_ref[:, g * kc:(g + 1) * kc]
        if conv_done is not None:
            xk = xk + _anchor_zero(conv_done)[0:1, 0:1].astype(xk.dtype)
        pv = jnp.dot(xk, wv_ref[g * kc:(g + 1) * kc, :], preferred_element_type=F32)
        pg = jnp.dot(xk, wg_ref[g * kc:(g + 1) * kc, :], preferred_element_type=F32)
        val = pv if val is None else val + pv
        gate = pg if gate is None else gate + pg

        prev = dot_done
        for r0 in row_starts[g * per_stage:(g + 1) * per_stage]:
            for c, sl in enumerate(lane_tiles):
                acc = jnp.broadcast_to(dwb_ref[:, sl], (rows, LANES))
                if prev is not None:
                    acc = acc + _anchor_zero(prev)[0:1, :]
                for k in range(kw):
                    acc = acc + dw_ref[k:k + 1, sl] * win_ref[c, base + r0 + k:base + r0 + k + rows, :]
                y_ref[r0:r0 + rows, sl] = acc
                prev = acc[0:SUBLANES, :]
        conv_done = prev
        dot_done = pg[tm - SUBLANES:tm, tn - LANES:tn]
    carry_ref[jc] = win_ref[:, tm:tm + CONV_HALO, :]

    h = (val + bv_ref[...]) * _sigmoid(gate + bg_ref[...])
    for c, sl in enumerate(lane_tiles):
        win_ref[c, CONV_HALO:CONV_HALO + tm, :] = h[:, sl]


def glu_dwconv(x, g, w, b, dw, dw_b, *, layer=0, tm_pref=512, tn_pref=512):
    s, d = x.shape
    n = w.shape[-1] // 2
    kw = dw.shape[0]
    tm, tn = _tile(s, tm_pref, CONV_HALO), _tile(n, tn_pref)
    assert kw - 1 <= CONV_HALO <= tm and tn % LANES == 0
    ni, nj = s // tm, n // tn
    total = ni * nj
    b2 = b.reshape(1, 2 * n)

    def proj(step):
        t = jnp.minimum(step, total - 1)
        return t // nj, t % nj

    def conv(step):
        t = jnp.maximum(step - 1, 0)
        return t // nj, t % nj

    return pl.pallas_call(
        functools.partial(_glu_conv_kernel, nj=nj),
        grid=(total + 1,),
        in_specs=[
            pl.BlockSpec((tm, d), lambda q: (proj(q)[0], 0)),
            pl.BlockSpec((1, d), lambda q: (0, 0)),
            _wspec(w, layer, (d, tn), lambda q: (0, proj(q)[1])),
            _wspec(w, layer, (d, tn), lambda q: (0, proj(q)[1] + nj)),
            pl.BlockSpec((1, tn), lambda q: (0, proj(q)[1])),
            pl.BlockSpec((1, tn), lambda q: (0, proj(q)[1] + nj)),
            pl.BlockSpec((kw, tn), lambda q: (0, conv(q)[1])),
            pl.BlockSpec((1, tn), lambda q: (0, conv(q)[1])),
        ],
        out_specs=pl.BlockSpec((tm, tn), lambda q: conv(q)),
        out_shape=jax.ShapeDtypeStruct((s, n), F32),
        scratch_shapes=[
            pltpu.VMEM((tm, d), BF16),
            pltpu.VMEM((tn // LANES, CONV_HALO + tm, LANES), F32),
            pltpu.VMEM((nj, tn // LANES, CONV_HALO, LANES), F32),
        ],
        compiler_params=_params(("arbitrary",)),
        name="glu_dwconv",
    )(x, g.reshape(1, d), w, w, b2, b2, dw, dw_b.reshape(1, n))


def _ln_silu_kernel(y_ref, lng_ref, lnb_ref, o_ref):
    y = y_ref[...]
    mu = jnp.mean(y, axis=-1, keepdims=True)
    yc = y - mu
    var = jnp.mean(yc * yc, axis=-1, keepdims=True)
    z = yc * lax.rsqrt(var + EPS) * lng_ref[...] + lnb_ref[...]
    o_ref[...] = (z * _sigmoid(z)).astype(o_ref.dtype)


def ln_silu(y, ln_g, ln_b, *, t_pref=256):
    s, d = y.shape
    t = _tile(s, t_pref, SUBLANES)
    return pl.pallas_call(
        _ln_silu_kernel,
        grid=(s // t,),
        in_specs=[
            pl.BlockSpec((t, d), lambda i: (i, 0)),
            pl.BlockSpec((1, d), lambda i: (0, 0)),
            pl.BlockSpec((1, d), lambda i: (0, 0)),
        ],
        out_specs=pl.BlockSpec((t, d), lambda i: (i, 0)),
        out_shape=jax.ShapeDtypeStruct((s, d), BF16),
        compiler_params=_params(("parallel",)),
        name="ln_silu",
    )(y, ln_g.reshape(1, d), ln_b.reshape(1, d))


LRU_HALO = SUBLANES


def _lru_kernel(recprev_ref, rec_ref, gb_ref, cw_ref, cb_ref, wa_ref, wx_ref, ba_ref, bx_ref,
                lam_ref, y_ref, win_ref, a_ref, b_ref, h_ref, carry_ref):
    i = pl.program_id(1)
    t, c = rec_ref.shape
    kw = cw_ref.shape[0]

    @pl.when(i == 0)
    def _():
        carry_ref[...] = jnp.zeros_like(carry_ref)

    keep = (i > 0).astype(F32)
    base = LRU_HALO - (kw - 1)
    pieces = []
    for ci, c0 in enumerate(range(0, c, LANES)):
        sl = slice(c0, c0 + LANES)
        win_ref[ci, 0:LRU_HALO, :] = recprev_ref[:, sl] * keep
        win_ref[ci, LRU_HALO:LRU_HALO + t, :] = rec_ref[:, sl]
        piece = jnp.broadcast_to(cb_ref[:, sl], (t, LANES))
        for k in range(kw):
            piece = piece + cw_ref[k:k + 1, sl] * win_ref[ci, base + k:base + k + t, :]
        pieces.append(piece)
    u = jnp.concatenate(pieces, axis=1)

    ub = u.astype(BF16)
    r = _sigmoid(jnp.dot(ub, wa_ref[0], preferred_element_type=F32) + ba_ref[...])
    ig = _sigmoid(jnp.dot(ub, wx_ref[0], preferred_element_type=F32) + bx_ref[...])
    log_a = (-LRU_C) * r * _softplus(-lam_ref[...])
    a_ref[...] = jnp.exp(log_a)
    b_ref[...] = jnp.sqrt(1.0 - jnp.exp(2.0 * log_a)) * (ig * u)

    def step(tt, h):
        h = a_ref[pl.ds(tt, 1), :] * h + b_ref[pl.ds(tt, 1), :]
        h_ref[pl.ds(tt, 1), :] = h
        return h

    carry_ref[...] = lax.fori_loop(0, t, step, carry_ref[...], unroll=8)
    y_ref[...] = (h_ref[...] * _gelu_tanh(gb_ref[...])).astype(y_ref.dtype)


def lru_core(hin, conv_w, conv_b, wa, wx, ba, bx, lam, *, t_pref=256):
    s = hin.shape[0]
    r = hin.shape[1] // 2
    ng, c, _ = wa.shape
    kw = conv_w.shape[0]
    assert kw - 1 <= LRU_HALO and ng * c == r
    t = _tile(s, t_pref, LRU_HALO)
    ratio = t // LRU_HALO
    row = lambda v: v.reshape(1, r)
    vec_spec = pl.BlockSpec((1, c), lambda g, i: (0, g))
    return pl.pallas_call(
        _lru_kernel,
        grid=(ng, s // t),
        in_specs=[
            pl.BlockSpec((LRU_HALO, c), lambda g, i: (jnp.maximum(i * ratio - 1, 0), ng + g)),
            pl.BlockSpec((t, c), lambda g, i: (i, ng + g)),
            pl.BlockSpec((t, c), lambda g, i: (i, g)),
            pl.BlockSpec((kw, c), lambda g, i: (0, g)),
            vec_spec,
            pl.BlockSpec((1, c, c), lambda g, i: (g, 0, 0)),
            pl.BlockSpec((1, c, c), lambda g, i: (g, 0, 0)),
            vec_spec, vec_spec, vec_spec,
        ],
        out_specs=pl.BlockSpec((t, c), lambda g, i: (i, g)),
        out_shape=jax.ShapeDtypeStruct((s, r), BF16),
        scratch_shapes=[
            pltpu.VMEM((c // LANES, LRU_HALO + t, LANES), F32),
            pltpu.VMEM((t, c), F32),
            pltpu.VMEM((t, c), F32),
            pltpu.VMEM((t, c), F32),
            pltpu.VMEM((1, c), F32),
        ],
        compiler_params=_params(("parallel", "arbitrary")),
        name="lru_core",
    )(hin, hin, hin, conv_w, row(conv_b), wa, wx, row(ba), row(bx), row(lam))


def _block_diag_groups(w, heads_per_group):
    h, hd, _ = w.shape
    g = heads_per_group
    wg = w.reshape(h // g, g, hd, hd)
    eye = jnp.eye(g, dtype=w.dtype)
    return jnp.einsum('ngij,gh->ngihj', wg, eye).reshape(h // g, g * hd, g * hd)


def _heads_per_group(n_heads, head_dim):
    g = 1
    while (g * head_dim) % LANES != 0 and g < n_heads:
        g += 1
    assert n_heads % g == 0
    return g


def _mlstm_kernel(q_ref, k_ref, v_ref, o_ref, gates_ref, gbias_ref, hg_ref, out_ref,
                  c_ref, n_ref, m_ref, *, n_heads):
    ci = pl.program_id(0)
    L = q_ref.shape[0]
    dqk = q_ref.shape[1] // n_heads
    dv = v_ref.shape[1] // n_heads
    scale = dqk ** -0.5

    @pl.when(ci == 0)
    def _():
        c_ref[...] = jnp.zeros_like(c_ref)
        n_ref[...] = jnp.zeros_like(n_ref)
        m_ref[...] = jnp.zeros_like(m_ref)

    gz = gates_ref[...] + gbias_ref[...]
    lane = lax.broadcasted_iota(jnp.int32, gz.shape, 1)
    glog = jnp.where(lane < n_heads, gz, -_softplus(-gz))
    glog_t = glog.T

    rowi = lax.broadcasted_iota(jnp.int32, (L, L), 0)
    coli = lax.broadcasted_iota(jnp.int32, (L, L), 1)
    causal = coli <= rowi

    for h in range(n_heads):
        li_col = glog[:, h:h + 1]
        li_row = glog_t[h:h + 1, :]
        lf_col = glog[:, n_heads + h:n_heads + h + 1]
        lf_row = glog_t[n_heads + h:n_heads + h + 1, :]
        b_col = jnp.sum(jnp.where(causal, lf_row, 0.0), axis=1, keepdims=True)
        b_row = jnp.sum(jnp.where(rowi <= coli, lf_col, 0.0), axis=0, keepdims=True)
        m_st = m_ref[h][0:1, 0:1]
        d_mat = jnp.where(causal, b_col - b_row + li_row, -jnp.inf)
        m_inter = b_col + m_st
        m_out = jnp.maximum(m_inter, jnp.max(d_mat, axis=1, keepdims=True))
        w_inter = jnp.exp(m_inter - m_out)

        qh = q_ref[:, h * dqk:(h + 1) * dqk]
        kh = k_ref[:, h * dqk:(h + 1) * dqk] * scale
        vh = v_ref[:, h * dv:(h + 1) * dv]
        qb, kb, vb = qh.astype(BF16), kh.astype(BF16), vh.astype(BF16)

        qk = lax.dot_general(qb, kb, (((1,), (1,)), ((), ())), preferred_element_type=F32)
        s_mat = qk * jnp.exp(d_mat - m_out)
        c_st = c_ref[h]
        n_st = n_ref[h]
        num = w_inter * jnp.dot(qb, c_st.astype(BF16), preferred_element_type=F32) \
            + jnp.dot(s_mat.astype(BF16), vb, preferred_element_type=F32)
        den = w_inter * jnp.sum(qh * n_st, axis=1, keepdims=True) + jnp.sum(s_mat, axis=1, keepdims=True)
        hh = num / jnp.maximum(jnp.abs(den), jnp.exp(-m_out))

        b_last = b_col[L - 1:L, :]
        a_s = b_last - b_col + li_col
        m_new = jnp.maximum(b_last + m_st, jnp.max(a_s, axis=0, keepdims=True))
        w_old = jnp.exp(b_last + m_st - m_new)
        w_s = jnp.exp(a_s - m_new)
        kw_ = kh * w_s
        c_ref[h] = w_old * c_st + lax.dot_general(kw_.astype(BF16), vb, (((0,), (0,)), ((), ())),
                                                  preferred_element_type=F32)
        n_ref[h] = w_old * n_st + jnp.sum(kw_, axis=0, keepdims=True)
        m_ref[h] = jnp.broadcast_to(m_new, m_ref.shape[1:])

        hn = hh * lax.rsqrt(jnp.mean(hh * hh, axis=-1, keepdims=True) + EPS) * hg_ref[:, h * dv:(h + 1) * dv]
        out_ref[:, h * dv:(h + 1) * dv] = (hn * _sigmoid(o_ref[:, h * dv:(h + 1) * dv])).astype(out_ref.dtype)


def mlstm_core(p, gates, gate_bias_row, head_g, *, n_heads, qk, v):
    s = p.shape[0]
    L = _tile(s, MLSTM_CHUNK, SUBLANES)
    assert (2 * qk) % v == 0
    voff = (2 * qk) // v
    dqk, dv = qk // n_heads, v // n_heads
    return pl.pallas_call(
        functools.partial(_mlstm_kernel, n_heads=n_heads),
        grid=(s // L,),
        in_specs=[
            pl.BlockSpec((L, qk), lambda c: (c, 0)),
            pl.BlockSpec((L, qk), lambda c: (c, 1)),
            pl.BlockSpec((L, v), lambda c: (c, voff)),
            pl.BlockSpec((L, v), lambda c: (c, voff + 1)),
            pl.BlockSpec((L, LANES), lambda c: (c, 0)),
            pl.BlockSpec((1, LANES), lambda c: (0, 0)),
            pl.BlockSpec((1, v), lambda c: (0, 0)),
        ],
        out_specs=pl.BlockSpec((L, v), lambda c: (c, 0)),
        out_shape=jax.ShapeDtypeStruct((s, v), BF16),
        scratch_shapes=[
            pltpu.VMEM((n_heads, dqk, dv), F32),
            pltpu.VMEM((n_heads, 1, dqk), F32),
            pltpu.VMEM((n_heads, SUBLANES, LANES), F32),
        ],
        compiler_params=_params(("arbitrary",)),
        name="mlstm_core",
    )(p, p, p, p, gates, gate_bias_row, head_g.reshape(1, v))


def _conv_layer(x, g_pre, g_post, layer, w_in, b_in, dw, dw_b, ln_g, ln_b, w_out, b_out):
    y = glu_dwconv(x, g_pre, w_in, b_in, dw, dw_b, layer=layer)
    a = ln_silu(y, ln_g, ln_b)
    return matmul_postnorm_residual(a, w_out, b_out, x, g_post, layer=layer)


def _lru_layer(x, g_pre, g_post, layer, w_in, b_in, conv_w, conv_b, gate_a_w, gate_x_w, gate_a_b, gate_x_b,
               lam, w_out, b_out):
    n_heads, head_dim, _ = gate_a_w.shape
    hpg = _heads_per_group(n_heads, head_dim)
    wa = _block_diag_groups(gate_a_w, hpg).astype(BF16)
    wx = _block_diag_groups(gate_x_w, hpg).astype(BF16)
    hin = norm_matmul(x, g_pre, w_in, b_in, layer=layer)
    y = lru_core(hin, conv_w, conv_b, wa, wx, gate_a_b, gate_x_b, lam)
    return matmul_postnorm_residual(y, w_out, b_out, x, g_post, layer=layer)


def _mlstm_layer(x, g_pre, g_post, w_in, gate_b, head_g, w_out):
    d = x.shape[1]
    n_heads = gate_b.shape[1]
    v = w_out.shape[0]
    proj = w_in.shape[1]
    qk = (proj - 2 * v - 2 * n_heads) // 2
    main = 2 * qk + 2 * v
    assert 2 * n_heads <= LANES
    w_bf = w_in.astype(BF16)
    w_gate = jnp.pad(w_bf[:, main:], ((0, 0), (0, LANES - 2 * n_heads)))
    p, gates = norm_matmul_with_side(x, g_pre, w_bf, jnp.zeros((main,), F32), w_gate)
    gbias = jnp.pad(gate_b.reshape(1, 2 * n_heads), ((0, 0), (0, LANES - 2 * n_heads)))
    hcore = mlstm_core(p, gates, gbias, head_g, n_heads=n_heads, qk=qk, v=v)
    return matmul_postnorm_residual(hcore, w_out.astype(BF16), jnp.zeros((d,), F32), x, g_post)


def kernel(x, norm_g, ffn_w_in, ffn_w_out, conv_w_in, conv_b_in, conv_dw, conv_dw_b, conv_ln_g, conv_ln_b, conv_w_out, conv_b_out, lru_w_in, lru_b_in, lru_conv_w, lru_conv_b, lru_gate_a_w, lru_gate_x_w, lru_gate_a_b, lru_gate_x_b, lru_lambda, lru_w_out, lru_b_out, mlstm_w_in, mlstm_gate_b, mlstm_head_g, mlstm_w_out):
    bsz, seq, d = x.shape
    assert bsz == 1, "sequence mixers are written for a single sequence"
    xs = x.reshape(seq, d)
    ffn_in, ffn_out = ffn_w_in.astype(BF16), ffn_w_out.astype(BF16)
    conv_in, conv_out = conv_w_in.astype(BF16), conv_w_out.astype(BF16)
    lru_in, lru_out = lru_w_in.astype(BF16), lru_w_out.astype(BF16)
    ia = ib = ic = 0
    for i in range(norm_g.shape[0]):
        kind = i % N_MIXERS
        g = norm_g[i]
        if kind == 0:
            xs = _conv_layer(xs, g[0], g[1], ia, conv_in, conv_b_in[ia], conv_dw[ia], conv_dw_b[ia],
                             conv_ln_g[ia], conv_ln_b[ia], conv_out, conv_b_out[ia])
            ia += 1
        elif kind == 1:
            xs = _lru_layer(xs, g[0], g[1], ib, lru_in, lru_b_in[ib], lru_conv_w[ib], lru_conv_b[ib],
                            lru_gate_a_w[ib], lru_gate_x_w[ib], lru_gate_a_b[ib], lru_gate_x_b[ib],
                            lru_lambda[ib], lru_out, lru_b_out[ib])
            ib += 1
        else:
            xs = _mlstm_layer(xs, g[0], g[1], mlstm_w_in[ic], mlstm_gate_b[ic], mlstm_head_g[ic],
                              mlstm_w_out[ic])
            ic += 1
        xs = ffn(xs, g[2], ffn_in, ffn_out, g[3], layer=i)
    return xs.reshape(bsz, seq, d)
```

```python
import functools
import math

import jax
import jax.numpy as jnp
from jax import lax
from jax.experimental import pallas as pl
from jax.experimental.pallas import tpu as pltpu

EPS = 1e-6
LRU_C = 8.0
N_MIXERS = 3
MLSTM_CHUNK = 256
LANES = 128
SUBLANES = 8
VMEM_LIMIT_BYTES = 56 * 1024 * 1024

F32 = jnp.float32
BF16 = jnp.bfloat16


def _tile(n, pref, align=LANES):
    if n <= pref:
        return n
    t = (pref // align) * align
    while t >= align:
        if n % t == 0:
            return t
        t -= align
    return n


def _wspec(w, layer, block, index_map, **kwargs):
    if w.ndim == 2:
        return pl.BlockSpec(block, index_map, **kwargs)
    return pl.BlockSpec((None,) + block, lambda *ids: (layer,) + index_map(*ids), **kwargs)


def _params(sem, vmem_limit_bytes=VMEM_LIMIT_BYTES):
    return pltpu.CompilerParams(dimension_semantics=sem, vmem_limit_bytes=vmem_limit_bytes)


def _sigmoid(z):
    return 1.0 / (1.0 + jnp.exp(-z))


def _softplus(z):
    return jnp.maximum(z, 0.0) + jnp.log1p(jnp.exp(-jnp.abs(z)))


def _gelu_tanh(z):
    c = math.sqrt(2.0 / math.pi)
    return 0.5 * z * (1.0 + jnp.tanh(c * (z + 0.044715 * (z * z * z))))


def _rms(y, g):
    ms = jnp.mean(y * y, axis=-1, keepdims=True)
    return y * lax.rsqrt(ms + EPS) * g


COL_CHUNK = 1024


def _col_chunks(d):
    step = min(COL_CHUNK, d)
    return [(c, min(step, d - c)) for c in range(0, d, step)]


def _inv_rms(load_chunk, d):
    ss = None
    for c, w in _col_chunks(d):
        y = load_chunk(c, w)
        part = jnp.sum(y * y, axis=-1, keepdims=True)
        ss = part if ss is None else ss + part
    return lax.rsqrt(ss * (1.0 / d) + EPS)


def _prenorm_to(xn_ref, x_ref, g_ref):
    d = x_ref.shape[1]
    inv = _inv_rms(lambda c, w: x_ref[:, c:c + w], d)
    for c, w in _col_chunks(d):
        xn_ref[:, c:c + w] = (x_ref[:, c:c + w] * inv * g_ref[:, c:c + w]).astype(xn_ref.dtype)


def _postnorm_residual_inplace(o_ref, xres_ref, g_ref):
    d = o_ref.shape[1]
    inv = _inv_rms(lambda c, w: o_ref[:, c:c + w], d)
    for c, w in _col_chunks(d):
        o_ref[:, c:c + w] = xres_ref[:, c:c + w] + o_ref[:, c:c + w] * inv * g_ref[:, c:c + w]


def _accumulate_dot(o_ref, a, w_ref):
    d = o_ref.shape[1]
    for c, w in _col_chunks(d):
        o_ref[:, c:c + w] += jnp.dot(a, w_ref[:, c:c + w], preferred_element_type=F32)


def _norm_mm_kernel(x_ref, g_ref, w_ref, b_ref, o_ref, xn_ref):
    @pl.when(pl.program_id(1) == 0)
    def _():
        _prenorm_to(xn_ref, x_ref, g_ref)

    o_ref[...] = jnp.dot(xn_ref[...], w_ref[...], preferred_element_type=F32) + b_ref[...]


def _norm_mm_side_kernel(x_ref, g_ref, w_ref, b_ref, ws_ref, o_ref, os_ref, xn_ref):
    @pl.when(pl.program_id(1) == 0)
    def _():
        _prenorm_to(xn_ref, x_ref, g_ref)
        os_ref[...] = jnp.dot(xn_ref[...], ws_ref[...], preferred_element_type=F32)

    o_ref[...] = jnp.dot(xn_ref[...], w_ref[...], preferred_element_type=F32) + b_ref[...]


def norm_matmul_with_side(x, g, w, b, w_side, *, tm_pref=512, tn_pref=1024):
    s, d = x.shape
    n = b.shape[0]
    ns = w_side.shape[1]
    tm, tn = _tile(s, tm_pref, SUBLANES), _tile(n, tn_pref)
    assert n % tn == 0 and n <= w.shape[1]
    return pl.pallas_call(
        _norm_mm_side_kernel,
        grid=(s // tm, n // tn),
        in_specs=[
            pl.BlockSpec((tm, d), lambda i, j: (i, 0)),
            pl.BlockSpec((1, d), lambda i, j: (0, 0)),
            pl.BlockSpec((d, tn), lambda i, j: (0, j)),
            pl.BlockSpec((1, tn), lambda i, j: (0, j)),
            pl.BlockSpec((d, ns), lambda i, j: (0, 0)),
        ],
        out_specs=[
            pl.BlockSpec((tm, tn), lambda i, j: (i, j)),
            pl.BlockSpec((tm, ns), lambda i, j: (i, 0)),
        ],
        out_shape=[jax.ShapeDtypeStruct((s, n), F32), jax.ShapeDtypeStruct((s, ns), F32)],
        scratch_shapes=[pltpu.VMEM((tm, d), BF16)],
        compiler_params=_params(("parallel", "arbitrary")),
        name="norm_matmul_with_side",
    )(x, g.reshape(1, d), w, b.reshape(1, n), w_side)


def norm_matmul(x, g, w, b, *, layer=0, tm_pref=512, tn_pref=1024):
    s, d = x.shape
    n = w.shape[-1]
    tm, tn = _tile(s, tm_pref, SUBLANES), _tile(n, tn_pref)
    return pl.pallas_call(
        _norm_mm_kernel,
        grid=(s // tm, n // tn),
        in_specs=[
            pl.BlockSpec((tm, d), lambda i, j: (i, 0)),
            pl.BlockSpec((1, d), lambda i, j: (0, 0)),
            _wspec(w, layer, (d, tn), lambda i, j: (0, j)),
            pl.BlockSpec((1, tn), lambda i, j: (0, j)),
        ],
        out_specs=pl.BlockSpec((tm, tn), lambda i, j: (i, j)),
        out_shape=jax.ShapeDtypeStruct((s, n), F32),
        scratch_shapes=[pltpu.VMEM((tm, d), BF16)],
        compiler_params=_params(("parallel", "arbitrary")),
        name="norm_matmul",
    )(x, g.reshape(1, d), w, b.reshape(1, n))


def _mm_postnorm_kernel(a_ref, w_ref, b_ref, xres_ref, g_ref, o_ref):
    k = pl.program_id(1)

    @pl.when(k == 0)
    def _():
        o_ref[...] = jnp.broadcast_to(b_ref[...], o_ref.shape)

    _accumulate_dot(o_ref, a_ref[...].astype(BF16), w_ref)

    @pl.when(k == pl.num_programs(1) - 1)
    def _():
        _postnorm_residual_inplace(o_ref, xres_ref, g_ref)


def _mm_postnorm_resident_kernel(a_ref, w_ref, b_ref, xres_ref, g_ref, o_ref):
    a = a_ref[...].astype(BF16)
    d = o_ref.shape[1]
    for c, w in _col_chunks(d):
        o_ref[:, c:c + w] = jnp.dot(a, w_ref[:, c:c + w], preferred_element_type=F32) + b_ref[:, c:c + w]
    _postnorm_residual_inplace(o_ref, xres_ref, g_ref)


RESIDENT_WEIGHT_MAX_BYTES = 32 * 1024 * 1024
RESIDENT_VMEM_LIMIT_BYTES = 60 * 1024 * 1024


def matmul_postnorm_residual(a, w, b, xres, g, *, layer=0, tm_pref=512, tk_pref=1024):
    s, kdim = a.shape
    d = w.shape[-1]
    if kdim * d * w.dtype.itemsize <= RESIDENT_WEIGHT_MAX_BYTES:
        tm = _tile(s, tm_pref // 2, SUBLANES)
        return pl.pallas_call(
            _mm_postnorm_resident_kernel,
            grid=(s // tm,),
            in_specs=[
                pl.BlockSpec((tm, kdim), lambda i: (i, 0)),
                _wspec(w, layer, (kdim, d), lambda i: (0, 0), pipeline_mode=pl.Buffered(1)),
                pl.BlockSpec((1, d), lambda i: (0, 0)),
                pl.BlockSpec((tm, d), lambda i: (i, 0)),
                pl.BlockSpec((1, d), lambda i: (0, 0)),
            ],
            out_specs=pl.BlockSpec((tm, d), lambda i: (i, 0)),
            out_shape=jax.ShapeDtypeStruct((s, d), F32),
            compiler_params=_params(("parallel",), RESIDENT_VMEM_LIMIT_BYTES),
            name="matmul_postnorm_residual_resident",
        )(a, w, b.reshape(1, d), xres, g.reshape(1, d))
    tm, tk = _tile(s, tm_pref, SUBLANES), _tile(kdim, tk_pref)
    return pl.pallas_call(
        _mm_postnorm_kernel,
        grid=(s // tm, kdim // tk),
        in_specs=[
            pl.BlockSpec((tm, tk), lambda i, k: (i, k)),
            _wspec(w, layer, (tk, d), lambda i, k: (k, 0)),
            pl.BlockSpec((1, d), lambda i, k: (0, 0)),
            pl.BlockSpec((tm, d), lambda i, k: (i, 0)),
            pl.BlockSpec((1, d), lambda i, k: (0, 0)),
        ],
        out_specs=pl.BlockSpec((tm, d), lambda i, k: (i, 0)),
        out_shape=jax.ShapeDtypeStruct((s, d), F32),
        compiler_params=_params(("parallel", "arbitrary")),
        name="matmul_postnorm_residual",
    )(a, w, b.reshape(1, d), xres, g.reshape(1, d))


def _ffn_kernel(x_ref, gpre_ref, wg_ref, wu_ref, wo_ref, gpost_ref, o_ref, xn_ref):
    f = pl.program_id(1)

    @pl.when(f == 0)
    def _():
        _prenorm_to(xn_ref, x_ref, gpre_ref)
        o_ref[...] = jnp.zeros_like(o_ref)

    xn = xn_ref[...]
    gate = jnp.dot(xn, wg_ref[...], preferred_element_type=F32)
    up = jnp.dot(xn, wu_ref[...], preferred_element_type=F32)
    act = (gate * _sigmoid(gate) * up).astype(BF16)
    _accumulate_dot(o_ref, act, wo_ref)

    @pl.when(f == pl.num_programs(1) - 1)
    def _():
        _postnorm_residual_inplace(o_ref, x_ref, gpost_ref)


def ffn(x, g_pre, w_in, w_out, g_post, *, layer=0, tm_pref=512, tf_pref=256):
    s, d = x.shape
    dff = w_out.shape[-2]
    tm, tf = _tile(s, tm_pref, SUBLANES), _tile(dff, tf_pref)
    nf = dff // tf
    return pl.pallas_call(
        _ffn_kernel,
        grid=(s // tm, nf),
        in_specs=[
            pl.BlockSpec((tm, d), lambda i, f: (i, 0)),
            pl.BlockSpec((1, d), lambda i, f: (0, 0)),
            _wspec(w_in, layer, (d, tf), lambda i, f: (0, f)),
            _wspec(w_in, layer, (d, tf), lambda i, f: (0, f + nf)),
            _wspec(w_out, layer, (tf, d), lambda i, f: (f, 0)),
            pl.BlockSpec((1, d), lambda i, f: (0, 0)),
        ],
        out_specs=pl.BlockSpec((tm, d), lambda i, f: (i, 0)),
        out_shape=jax.ShapeDtypeStruct((s, d), F32),
        scratch_shapes=[pltpu.VMEM((tm, d), BF16)],
        compiler_params=_params(("parallel", "arbitrary")),
        name="ffn",
    )(x, g_pre.reshape(1, d), w_in, w_in, w_out, g_post.reshape(1, d))


CONV_HALO = 32
CONV_ROWS = 32
LN_OUTPROJ_ROWS = 128
GLU_CONV_STAGES = 8
MXU_K_TILE = 256


def _anchor_zero(v):
    u = pltpu.bitcast(v, jnp.uint32)
    z = lax.shift_right_logical(lax.shift_right_logical(u, jnp.uint32(16)), jnp.uint32(16))
    return pltpu.bitcast(z, F32)


def _glu_conv_kernel(x_ref, g_ref, wv_ref, wg_ref, bv_ref, bg_ref, dw_ref, dwb_ref, y_ref,
                     xn_ref, win_ref, carry_ref, *, nj):
    s = pl.program_id(0)
    total = pl.num_programs(0) - 1
    tm = x_ref.shape[0]
    tn = wv_ref.shape[1]
    kw = dw_ref.shape[0]
    lane_tiles = [slice(c, c + LANES) for c in range(0, tn, LANES)]

    @pl.when(s == 0)
    def _():
        win_ref[...] = jnp.zeros_like(win_ref)
        carry_ref[...] = jnp.zeros_like(carry_ref)

    @pl.when(jnp.logical_and(s % nj == 0, s < total))
    def _():
        _prenorm_to(xn_ref, x_ref, g_ref)

    jc = jnp.maximum(s - 1, 0) % nj
    win_ref[:, 0:CONV_HALO, :] = carry_ref[jc]
    base = CONV_HALO - (kw - 1)
    rows = min(CONV_ROWS, tm)
    row_starts = list(range(0, tm, rows))
    d = xn_ref.shape[1]
    stages = max(1, min(GLU_CONV_STAGES, len(row_starts), d // MXU_K_TILE))
    assert len(row_starts) % stages == 0 and d % stages == 0
    per_stage = len(row_starts) // stages
    kc = d // stages
    val = gate = None
    conv_done = dot_done = None
    for g in range(stages):
        xk = xn_ref[:, g * kc:(g + 1) * kc]
        if conv_done is not None:
            xk = xk + _anchor_zero(conv_done)[0:1, 0:1].astype(xk.dtype)
        pv = jnp.dot(xk, wv_ref[g * kc:(g + 1) * kc, :], preferred_element_type=F32)
        pg = jnp.dot(xk, wg_ref[g * kc:(g + 1) * kc, :], preferred_element_type=F32)
        val = pv if val is None else val + pv
        gate = pg if gate is None else gate + pg

        prev = dot_done
        for r0 in row_starts[g * per_stage:(g + 1) * per_stage]:
            for c, sl in enumerate(lane_tiles):
                acc = jnp.broadcast_to(dwb_ref[:, sl], (rows, LANES))
                if prev is not None:
                    acc = acc + _anchor_zero(prev)[0:1, :]
                for k in range(kw):
                    acc = acc + dw_ref[k:k + 1, sl] * win_ref[c, base + r0 + k:base + r0 + k + rows, :]
                y_ref[r0:r0 + rows, sl] = acc
                prev = acc[0:SUBLANES, :]
        conv_done = prev
        dot_done = pg[tm - SUBLANES:tm, tn - LANES:tn]
    carry_ref[jc] = win_ref[:, tm:tm + CONV_HALO, :]

    h = (val + bv_ref[...]) * _sigmoid(gate + bg_ref[...])
    for c, sl in enumerate(lane_tiles):
        win_ref[c, CONV_HALO:CONV_HALO + tm, :] = h[:, sl]


def glu_dwconv(x, g, w, b, dw, dw_b, *, layer=0, tm_pref=512, tn_pref=512):
    s, d = x.shape
    n = w.shape[-1] // 2
    kw = dw.shape[0]
    tm, tn = _tile(s, tm_pref, CONV_HALO), _tile(n, tn_pref)
    assert kw - 1 <= CONV_HALO <= tm and tn % LANES == 0
    ni, nj = s // tm, n // tn
    total = ni * nj
    b2 = b.reshape(1, 2 * n)

    def proj(step):
        t = jnp.minimum(step, total - 1)
        return t // nj, t % nj

    def conv(step):
        t = jnp.maximum(step - 1, 0)
        return t // nj, t % nj

    return pl.pallas_call(
        functools.partial(_glu_conv_kernel, nj=nj),
        grid=(total + 1,),
        in_specs=[
            pl.BlockSpec((tm, d), lambda q: (proj(q)[0], 0)),
            pl.BlockSpec((1, d), lambda q: (0, 0)),
            _wspec(w, layer, (d, tn), lambda q: (0, proj(q)[1])),
            _wspec(w, layer, (d, tn), lambda q: (0, proj(q)[1] + nj)),
            pl.BlockSpec((1, tn), lambda q: (0, proj(q)[1])),
            pl.BlockSpec((1, tn), lambda q: (0, proj(q)[1] + nj)),
            pl.BlockSpec((kw, tn), lambda q: (0, conv(q)[1])),
            pl.BlockSpec((1, tn), lambda q: (0, conv(q)[1])),
        ],
        out_specs=pl.BlockSpec((tm, tn), lambda q: conv(q)),
        out_shape=jax.ShapeDtypeStruct((s, n), F32),
        scratch_shapes=[
            pltpu.VMEM((tm, d), BF16),
            pltpu.VMEM((tn // LANES, CONV_HALO + tm, LANES), F32),
            pltpu.VMEM((nj, tn // LANES, CONV_HALO, LANES), F32),
        ],
        compiler_params=_params(("arbitrary",)),
        name="glu_dwconv",
    )(x, g.reshape(1, d), w, w, b2, b2, dw, dw_b.reshape(1, n))


def _ln_silu_mm_postnorm_kernel(y_ref, lng_ref, lnb_ref, w_ref, b_ref, xres_ref, g_ref, o_ref, a_ref):
    y = y_ref[...]
    mu = jnp.mean(y, axis=-1, keepdims=True)
    yc = y - mu
    var = jnp.mean(yc * yc, axis=-1, keepdims=True)
    z = yc * lax.rsqrt(var + EPS) * lng_ref[...] + lnb_ref[...]
    a_ref[...] = (z * _sigmoid(z)).astype(a_ref.dtype)
    a = a_ref[...]
    d = o_ref.shape[1]
    for c, w in _col_chunks(d):
        o_ref[:, c:c + w] = jnp.dot(a, w_ref[:, c:c + w], preferred_element_type=F32) + b_ref[:, c:c + w]
    _postnorm_residual_inplace(o_ref, xres_ref, g_ref)


def ln_silu_matmul_postnorm_residual(y, ln_g, ln_b, w, b, xres, g, *, layer=0, tm_pref=LN_OUTPROJ_ROWS):
    s, kdim = y.shape
    d = w.shape[-1]
    tm = _tile(s, tm_pref, SUBLANES)
    return pl.pallas_call(
        _ln_silu_mm_postnorm_kernel,
        grid=(s // tm,),
        in_specs=[
            pl.BlockSpec((tm, kdim), lambda i: (i, 0)),
            pl.BlockSpec((1, kdim), lambda i: (0, 0)),
            pl.BlockSpec((1, kdim), lambda i: (0, 0)),
            _wspec(w, layer, (kdim, d), lambda i: (0, 0), pipeline_mode=pl.Buffered(1)),
            pl.BlockSpec((1, d), lambda i: (0, 0)),
            pl.BlockSpec((tm, d), lambda i: (i, 0)),
            pl.BlockSpec((1, d), lambda i: (0, 0)),
        ],
        out_specs=pl.BlockSpec((tm, d), lambda i: (i, 0)),
        out_shape=jax.ShapeDtypeStruct((s, d), F32),
        scratch_shapes=[pltpu.VMEM((tm, kdim), BF16)],
        compiler_params=_params(("parallel",), RESIDENT_VMEM_LIMIT_BYTES),
        name="ln_silu_matmul_postnorm_residual",
    )(y, ln_g.reshape(1, kdim), ln_b.reshape(1, kdim), w, b.reshape(1, d), xres, g.reshape(1, d))


def _ln_silu_kernel(y_ref, lng_ref, lnb_ref, o_ref):
    y = y_ref[...]
    mu = jnp.mean(y, axis=-1, keepdims=True)
    yc = y - mu
    var = jnp.mean(yc * yc, axis=-1, keepdims=True)
    z = yc * lax.rsqrt(var + EPS) * lng_ref[...] + lnb_ref[...]
    o_ref[...] = (z * _sigmoid(z)).astype(o_ref.dtype)


def ln_silu(y, ln_g, ln_b, *, t_pref=256):
    s, d = y.shape
    t = _tile(s, t_pref, SUBLANES)
    return pl.pallas_call(
        _ln_silu_kernel,
        grid=(s // t,),
        in_specs=[
            pl.BlockSpec((t, d), lambda i: (i, 0)),
            pl.BlockSpec((1, d), lambda i: (0, 0)),
            pl.BlockSpec((1, d), lambda i: (0, 0)),
        ],
        out_specs=pl.BlockSpec((t, d), lambda i: (i, 0)),
        out_shape=jax.ShapeDtypeStruct((s, d), BF16),
        compiler_params=_params(("parallel",)),
        name="ln_silu",
    )(y, ln_g.reshape(1, d), ln_b.reshape(1, d))


LRU_HALO = SUBLANES


def _lru_kernel(recprev_ref, rec_ref, gb_ref, cw_ref, cb_ref, wa_ref, wx_ref, ba_ref, bx_ref,
                lam_ref, y_ref, win_ref, a_ref, b_ref, h_ref, carry_ref):
    i = pl.program_id(1)
    t, c = rec_ref.shape
    kw = cw_ref.shape[0]

    @pl.when(i == 0)
    def _():
        carry_ref[...] = jnp.zeros_like(carry_ref)

    keep = (i > 0).astype(F32)
    base = LRU_HALO - (kw - 1)
    pieces = []
    for ci, c0 in enumerate(range(0, c, LANES)):
        sl = slice(c0, c0 + LANES)
        win_ref[ci, 0:LRU_HALO, :] = recprev_ref[:, sl] * keep
        win_ref[ci, LRU_HALO:LRU_HALO + t, :] = rec_ref[:, sl]
        piece = jnp.broadcast_to(cb_ref[:, sl], (t, LANES))
        for k in range(kw):
            piece = piece + cw_ref[k:k + 1, sl] * win_ref[ci, base + k:base + k + t, :]
        pieces.append(piece)
    u = jnp.concatenate(pieces, axis=1)

    ub = u.astype(BF16)
    r = _sigmoid(jnp.dot(ub, wa_ref[0], preferred_element_type=F32) + ba_ref[...])
    ig = _sigmoid(jnp.dot(ub, wx_ref[0], preferred_element_type=F32) + bx_ref[...])
    log_a = (-LRU_C) * r * _softplus(-lam_ref[...])
    a_ref[...] = jnp.exp(log_a)
    b_ref[...] = jnp.sqrt(1.0 - jnp.exp(2.0 * log_a)) * (ig * u)

    def step(tt, h):
        h = a_ref[pl.ds(tt, 1), :] * h + b_ref[pl.ds(tt, 1), :]
        h_ref[pl.ds(tt, 1), :] = h
        return h

    carry_ref[...] = lax.fori_loop(0, t, step, carry_ref[...], unroll=8)
    y_ref[...] = (h_ref[...] * _gelu_tanh(gb_ref[...])).astype(y_ref.dtype)


def lru_core(hin, conv_w, conv_b, wa, wx, ba, bx, lam, *, t_pref=256):
    s = hin.shape[0]
    r = hin.shape[1] // 2
    ng, c, _ = wa.shape
    kw = conv_w.shape[0]
    assert kw - 1 <= LRU_HALO and ng * c == r
    t = _tile(s, t_pref, LRU_HALO)
    ratio = t // LRU_HALO
    row = lambda v: v.reshape(1, r)
    vec_spec = pl.BlockSpec((1, c), lambda g, i: (0, g))
    return pl.pallas_call(
        _lru_kernel,
        grid=(ng, s // t),
        in_specs=[
            pl.BlockSpec((LRU_HALO, c), lambda g, i: (jnp.maximum(i * ratio - 1, 0), ng + g)),
            pl.BlockSpec((t, c), lambda g, i: (i, ng + g)),
            pl.BlockSpec((t, c), lambda g, i: (i, g)),
            pl.BlockSpec((kw, c), lambda g, i: (0, g)),
            vec_spec,
            pl.BlockSpec((1, c, c), lambda g, i: (g, 0, 0)),
            pl.BlockSpec((1, c, c), lambda g, i: (g, 0, 0)),
            vec_spec, vec_spec, vec_spec,
        ],
        out_specs=pl.BlockSpec((t, c), lambda g, i: (i, g)),
        out_shape=jax.ShapeDtypeStruct((s, r), BF16),
        scratch_shapes=[
            pltpu.VMEM((c // LANES, LRU_HALO + t, LANES), F32),
            pltpu.VMEM((t, c), F32),
            pltpu.VMEM((t, c), F32),
            pltpu.VMEM((t, c), F32),
            pltpu.VMEM((1, c), F32),
        ],
        compiler_params=_params(("parallel", "arbitrary")),
        name="lru_core",
    )(hin, hin, hin, conv_w, row(conv_b), wa, wx, row(ba), row(bx), row(lam))


def _block_diag_groups(w, heads_per_group):
    h, hd, _ = w.shape
    g = heads_per_group
    wg = w.reshape(h // g, g, hd, hd)
    eye = jnp.eye(g, dtype=w.dtype)
    return jnp.einsum('ngij,gh->ngihj', wg, eye).reshape(h // g, g * hd, g * hd)


def _heads_per_group(n_heads, head_dim):
    g = 1
    while (g * head_dim) % LANES != 0 and g < n_heads:
        g += 1
    assert n_heads % g == 0
    return g


def _mlstm_kernel(q_ref, k_ref, v_ref, o_ref, gates_ref, gbias_ref, hg_ref, out_ref,
                  c_ref, n_ref, m_ref, *, n_heads):
    ci = pl.program_id(0)
    L = q_ref.shape[0]
    dqk = q_ref.shape[1] // n_heads
    dv = v_ref.shape[1] // n_heads
    scale = dqk ** -0.5

    @pl.when(ci == 0)
    def _():
        c_ref[...] = jnp.zeros_like(c_ref)
        n_ref[...] = jnp.zeros_like(n_ref)
        m_ref[...] = jnp.zeros_like(m_ref)

    gz = gates_ref[...] + gbias_ref[...]
    lane = lax.broadcasted_iota(jnp.int32, gz.shape, 1)
    glog = jnp.where(lane < n_heads, gz, -_softplus(-gz))
    glog_t = glog.T

    rowi = lax.broadcasted_iota(jnp.int32, (L, L), 0)
    coli = lax.broadcasted_iota(jnp.int32, (L, L), 1)
    causal = coli <= rowi

    for h in range(n_heads):
        li_col = glog[:, h:h + 1]
        li_row = glog_t[h:h + 1, :]
        lf_col = glog[:, n_heads + h:n_heads + h + 1]
        lf_row = glog_t[n_heads + h:n_heads + h + 1, :]
        b_col = jnp.sum(jnp.where(causal, lf_row, 0.0), axis=1, keepdims=True)
        b_row = jnp.sum(jnp.where(rowi <= coli, lf_col, 0.0), axis=0, keepdims=True)
        m_st = m_ref[h][0:1, 0:1]
        d_mat = jnp.where(causal, b_col - b_row + li_row, -jnp.inf)
        m_inter = b_col + m_st
        m_out = jnp.maximum(m_inter, jnp.max(d_mat, axis=1, keepdims=True))
        w_inter = jnp.exp(m_inter - m_out)

        qh = q_ref[:, h * dqk:(h + 1) * dqk]
        kh = k_ref[:, h * dqk:(h + 1) * dqk] * scale
        vh = v_ref[:, h * dv:(h + 1) * dv]
        qb, kb, vb = qh.astype(BF16), kh.astype(BF16), vh.astype(BF16)

        qk = lax.dot_general(qb, kb, (((1,), (1,)), ((), ())), preferred_element_type=F32)
        s_mat = qk * jnp.exp(d_mat - m_out)
        c_st = c_ref[h]
        n_st = n_ref[h]
        num = w_inter * jnp.dot(qb, c_st.astype(BF16), preferred_element_type=F32) \
            + jnp.dot(s_mat.astype(BF16), vb, preferred_element_type=F32)
        den = w_inter * jnp.sum(qh * n_st, axis=1, keepdims=True) + jnp.sum(s_mat, axis=1, keepdims=True)
        hh = num / jnp.maximum(jnp.abs(den), jnp.exp(-m_out))

        b_last = b_col[L - 1:L, :]
        a_s = b_last - b_col + li_col
        m_new = jnp.maximum(b_last + m_st, jnp.max(a_s, axis=0, keepdims=True))
        w_old = jnp.exp(b_last + m_st - m_new)
        w_s = jnp.exp(a_s - m_new)
        kw_ = kh * w_s
        c_ref[h] = w_old * c_st + lax.dot_general(kw_.astype(BF16), vb, (((0,), (0,)), ((), ())),
                                                  preferred_element_type=F32)
        n_ref[h] = w_old * n_st + jnp.sum(kw_, axis=0, keepdims=True)
        m_ref[h] = jnp.broadcast_to(m_new, m_ref.shape[1:])

        hn = hh * lax.rsqrt(jnp.mean(hh * hh, axis=-1, keepdims=True) + EPS) * hg_ref[:, h * dv:(h + 1) * dv]
        out_ref[:, h * dv:(h + 1) * dv] = (hn * _sigmoid(o_ref[:, h * dv:(h + 1) * dv])).astype(out_ref.dtype)


def mlstm_core(p, gates, gate_bias_row, head_g, *, n_heads, qk, v):
    s = p.shape[0]
    L = _tile(s, MLSTM_CHUNK, SUBLANES)
    assert (2 * qk) % v == 0
    voff = (2 * qk) // v
    dqk, dv = qk // n_heads, v // n_heads
    return pl.pallas_call(
        functools.partial(_mlstm_kernel, n_heads=n_heads),
        grid=(s // L,),
        in_specs=[
            pl.BlockSpec((L, qk), lambda c: (c, 0)),
            pl.BlockSpec((L, qk), lambda c: (c, 1)),
            pl.BlockSpec((L, v), lambda c: (c, voff)),
            pl.BlockSpec((L, v), lambda c: (c, voff + 1)),
            pl.BlockSpec((L, LANES), lambda c: (c, 0)),
            pl.BlockSpec((1, LANES), lambda c: (0, 0)),
            pl.BlockSpec((1, v), lambda c: (0, 0)),
        ],
        out_specs=pl.BlockSpec((L, v), lambda c: (c, 0)),
        out_shape=jax.ShapeDtypeStruct((s, v), BF16),
        scratch_shapes=[
            pltpu.VMEM((n_heads, dqk, dv), F32),
            pltpu.VMEM((n_heads, 1, dqk), F32),
            pltpu.VMEM((n_heads, SUBLANES, LANES), F32),
        ],
        compiler_params=_params(("arbitrary",)),
        name="mlstm_core",
    )(p, p, p, p, gates, gate_bias_row, head_g.reshape(1, v))


def _conv_layer(x, g_pre, g_post, layer, w_in, b_in, dw, dw_b, ln_g, ln_b, w_out, b_out):
    y = glu_dwconv(x, g_pre, w_in, b_in, dw, dw_b, layer=layer)
    return ln_silu_matmul_postnorm_residual(y, ln_g, ln_b, w_out, b_out, x, g_post, layer=layer)


def _lru_layer(x, g_pre, g_post, layer, w_in, b_in, conv_w, conv_b, gate_a_w, gate_x_w, gate_a_b, gate_x_b,
               lam, w_out, b_out):
    n_heads, head_dim, _ = gate_a_w.shape
    hpg = _heads_per_group(n_heads, head_dim)
    wa = _block_diag_groups(gate_a_w, hpg).astype(BF16)
    wx = _block_diag_groups(gate_x_w, hpg).astype(BF16)
    hin = norm_matmul(x, g_pre, w_in, b_in, layer=layer)
    y = lru_core(hin, conv_w, conv_b, wa, wx, gate_a_b, gate_x_b, lam)
    return matmul_postnorm_residual(y, w_out, b_out, x, g_post, layer=layer)


def _mlstm_layer(x, g_pre, g_post, w_in, gate_b, head_g, w_out):
    d = x.shape[1]
    n_heads = gate_b.shape[1]
    v = w_out.shape[0]
    proj = w_in.shape[1]
    qk = (proj - 2 * v - 2 * n_heads) // 2
    main = 2 * qk + 2 * v
    assert 2 * n_heads <= LANES
    w_bf = w_in.astype(BF16)
    w_gate = jnp.pad(w_bf[:, main:], ((0, 0), (0, LANES - 2 * n_heads)))
    p, gates = norm_matmul_with_side(x, g_pre, w_bf, jnp.zeros((main,), F32), w_gate)
    gbias = jnp.pad(gate_b.reshape(1, 2 * n_heads), ((0, 0), (0, LANES - 2 * n_heads)))
    hcore = mlstm_core(p, gates, gbias, head_g, n_heads=n_heads, qk=qk, v=v)
    return matmul_postnorm_residual(hcore, w_out.astype(BF16), jnp.zeros((d,), F32), x, g_post)


def kernel(x, norm_g, ffn_w_in, ffn_w_out, conv_w_in, conv_b_in, conv_dw, conv_dw_b, conv_ln_g, conv_ln_b, conv_w_out, conv_b_out, lru_w_in, lru_b_in, lru_conv_w, lru_conv_b, lru_gate_a_w, lru_gate_x_w, lru_gate_a_b, lru_gate_x_b, lru_lambda, lru_w_out, lru_b_out, mlstm_w_in, mlstm_gate_b, mlstm_head_g, mlstm_w_out):
    bsz, seq, d = x.shape
    assert bsz == 1, "sequence mixers are written for a single sequence"
    xs = x.reshape(seq, d)
    ffn_in, ffn_out = ffn_w_in.astype(BF16), ffn_w_out.astype(BF16)
    conv_in, conv_out = conv_w_in.astype(BF16), conv_w_out.astype(BF16)
    lru_in, lru_out = lru_w_in.astype(BF16), lru_w_out.astype(BF16)
    ia = ib = ic = 0
    for i in range(norm_g.shape[0]):
        kind = i % N_MIXERS
        g = norm_g[i]
        if kind == 0:
            xs = _conv_layer(xs, g[0], g[1], ia, conv_in, conv_b_in[ia], conv_dw[ia], conv_dw_b[ia],
                             conv_ln_g[ia], conv_ln_b[ia], conv_out, conv_b_out[ia])
            ia += 1
        elif kind == 1:
            xs = _lru_layer(xs, g[0], g[1], ib, lru_in, lru_b_in[ib], lru_conv_w[ib], lru_conv_b[ib],
                            lru_gate_a_w[ib], lru_gate_x_w[ib], lru_gate_a_b[ib], lru_gate_x_b[ib],
                            lru_lambda[ib], lru_out, lru_b_out[ib])
            ib += 1
        else:
            xs = _mlstm_layer(xs, g[0], g[1], mlstm_w_in[ic], mlstm_gate_b[ic], mlstm_head_g[ic],
                              mlstm_w_out[ic])
            ic += 1
        xs = ffn(xs, g[2], ffn_in, ffn_out, g[3], layer=i)
    return xs.reshape(bsz, seq, d)
```
